```python
import math
import jax, jax.numpy as jnp
from jax import lax
import numpy as np

D_MODEL = 1024
BATCH = 8
SEQ = 2048
DEPTH = 2
DEC_BATCH = 2
DEC_SEQ = 8192
PAST_LEN = 128

HEAD_DIM = 64
N_HEADS_A = 8
N_HEADS_B = 4
WIDTH_A = N_HEADS_A * HEAD_DIM
WIDTH_B = N_HEADS_B * 2 * HEAD_DIM
MIX_WIDTH = WIDTH_A + WIDTH_B
IN_WIDTH = 3 * WIDTH_A + 3 * WIDTH_B
PATTERNS = ((128, 1), (512, 4), (2048, 16))
N_BUCKETS = 32
MAX_DISTANCE = 1024
N_BIAS_HEADS = N_HEADS_A + N_HEADS_B
Q_BLOCK = 128
D_FF = ((8 * D_MODEL // 3 + 255) // 256) * 256
EPS = 1e-6
NEG = -1e30

kernel_name = "hybrid_dilated_diff_attn_encoder"


def rmsnorm(x, g):
    xf = x.astype(jnp.float32)
    y = xf * lax.rsqrt(jnp.mean(xf * xf, axis=-1, keepdims=True) + EPS)
    return (y * g.astype(jnp.float32)).astype(x.dtype)


def t5_bucket(rel):
    nb = N_BUCKETS // 2
    max_exact = nb // 2
    ret = (rel > 0).astype(jnp.int32) * nb
    n = jnp.abs(rel)
    nf = jnp.maximum(n, 1).astype(jnp.float32)
    large = max_exact + (jnp.log(nf / max_exact) / math.log(MAX_DISTANCE / max_exact)
                         * (nb - max_exact)).astype(jnp.int32)
    large = jnp.minimum(large, nb - 1)
    return ret + jnp.where(n < max_exact, n, large)


def dilated_pattern(q, k, v, table_a, window, dilation):
    B, L, H, Dh = q.shape
    d = dilation
    n = L // d
    half = window // (2 * d)
    blk = half
    nb = -(-n // blk)
    npad = nb * blk

    def to_sub(t):
        t = t.reshape(B, n, d, H, Dh).transpose(0, 2, 1, 3, 4).reshape(B * d, n, H, Dh)
        return jnp.pad(t, ((0, 0), (0, npad - n), (0, 0), (0, 0)))

    def windows(t):
        tp = jnp.pad(t, ((0, 0), (blk, blk), (0, 0), (0, 0))).reshape(B * d, nb + 2, blk, H, Dh)
        return jnp.concatenate([tp[:, :-2], tp[:, 1:-1], tp[:, 2:]], axis=2)

    qs = to_sub(q).reshape(B * d, nb, blk, H, Dh)
    ks = windows(to_sub(k))
    vs = windows(to_sub(v))
    s = jnp.einsum('zbqhd,zbkhd->zbhqk', qs, ks,
                   preferred_element_type=jnp.float32) * (Dh ** -0.5)
    rel = jnp.arange(3 * blk)[None, :] - blk - jnp.arange(blk)[:, None]
    bias = table_a.astype(jnp.float32)[t5_bucket(rel * d)].transpose(2, 0, 1)
    kidx = jnp.arange(nb)[:, None] * blk + jnp.arange(3 * blk)[None, :] - blk
    valid = (kidx >= 0)[:, None, :] & (kidx < n)[:, None, :] & (jnp.abs(rel) <= half)[None]
    s = jnp.where(valid[None, :, None], s + bias[None, None], NEG)
    m = jnp.max(s, axis=-1, keepdims=True)
    p = jnp.exp(s - m)
    den = jnp.sum(p, axis=-1, keepdims=True)
    o = jnp.einsum('zbhqk,zbkhd->zbqhd', (p / den).astype(v.dtype), vs)
    lse = (m + jnp.log(den))[..., 0].transpose(0, 1, 3, 2)
    o = o.reshape(B * d, npad, H, Dh)[:, :n].reshape(B, d, n, H, Dh)
    o = o.transpose(0, 2, 1, 3, 4).reshape(B, L, H, Dh)
    lse = lse.reshape(B * d, npad, H)[:, :n].reshape(B, d, n, H).transpose(0, 2, 1, 3).reshape(B, L, H)
    return o, lse


def dilated_attention(q, k, v, table_a):
    outs, lses = [], []
    for window, dilation in PATTERNS:
        o, lse = dilated_pattern(q, k, v, table_a, window, dilation)
        outs.append(o)
        lses.append(lse)
    w = jax.nn.softmax(jnp.stack(lses, axis=0), axis=0)
    o = jnp.sum(w[..., None] * jnp.stack(outs, axis=0).astype(jnp.float32), axis=0)
    return o.astype(q.dtype)


def diff_attention(q, k, v, table_b, lam, lam_init, subln_g):
    B, L = q.shape[0], q.shape[1]
    nq = L // Q_BLOCK
    scale = HEAD_DIM ** -0.5
    tb = table_b.astype(jnp.float32)
    q_blocks = q.reshape(B, nq, Q_BLOCK, 2, N_HEADS_B, HEAD_DIM).transpose(1, 0, 2, 3, 4, 5)
    starts = jnp.arange(nq, dtype=jnp.int32) * Q_BLOCK
    kpos = jnp.arange(L, dtype=jnp.int32)

    def block(args):
        qb, start = args
        s = jnp.einsum('bqmhd,bkmhd->bmhqk', qb, k,
                       preferred_element_type=jnp.float32) * scale
        rel = kpos[None, :] - (start + jnp.arange(Q_BLOCK, dtype=jnp.int32))[:, None]
        bias = tb[t5_bucket(rel)].transpose(2, 0, 1)
        p = jax.nn.softmax(s + bias[None, None], axis=-1)
        a = p[:, 0] - lam * p[:, 1]
        return jnp.einsum('bhqk,bkhe->bqhe', a.astype(v.dtype), v)

    o = lax.map(block, (q_blocks, starts))
    o = o.transpose(1, 0, 2, 3, 4).reshape(B, L, N_HEADS_B, 2 * HEAD_DIM)
    o = rmsnorm(o, subln_g) * (1.0 - lam_init)
    return o.reshape(B, L, WIDTH_B)


def modulate(x, g, shift, scale):
    return rmsnorm(x, g) * (1.0 + scale[:, None, :]) + shift[:, None, :]


def trunk(x, c, rel_bias, ada_w, ada_b, norm_mix_g, norm_ffn_g, w_in, w_out,
          lambda_q1, lambda_k1, lambda_q2, lambda_k2, subln_g, w_gate, w_up, w_down, final_g):
    B, L, _ = x.shape
    table_a = rel_bias[:, :N_HEADS_A]
    table_b = rel_bias[:, N_HEADS_A:]
    for l in range(DEPTH):
        mod = jax.nn.silu(c) @ ada_w[l] + ada_b[l]
        sh1, sc1, g1, sh2, sc2, g2 = jnp.split(mod, 6, axis=-1)
        h = modulate(x, norm_mix_g[l], sh1, sc1)
        proj = h @ w_in[l]
        qa, ka, va, qb, kb, vb = jnp.split(
            proj, [WIDTH_A, 2 * WIDTH_A, 3 * WIDTH_A,
                   3 * WIDTH_A + WIDTH_B, 3 * WIDTH_A + 2 * WIDTH_B], axis=-1)
        qa = qa.reshape(B, L, N_HEADS_A, HEAD_DIM)
        ka = ka.reshape(B, L, N_HEADS_A, HEAD_DIM)
        va = va.reshape(B, L, N_HEADS_A, HEAD_DIM)
        oa = dilated_attention(qa, ka, va, table_a).reshape(B, L, WIDTH_A)
        lam_init = 0.8 - 0.6 * math.exp(-0.3 * l)
        lam = (jnp.exp(jnp.sum(lambda_q1[l].astype(jnp.float32) * lambda_k1[l].astype(jnp.float32)))
               - jnp.exp(jnp.sum(lambda_q2[l].astype(jnp.float32) * lambda_k2[l].astype(jnp.float32)))
               + lam_init)
        qb = qb.reshape(B, L, 2, N_HEADS_B, HEAD_DIM)
        kb = kb.reshape(B, L, 2, N_HEADS_B, HEAD_DIM)
        vb = vb.reshape(B, L, N_HEADS_B, 2 * HEAD_DIM)
        ob = diff_attention(qb, kb, vb, table_b, lam, lam_init, subln_g[l])
        mix = jnp.concatenate([oa, ob], axis=-1) @ w_out[l]
        x = x + g1[:, None, :] * mix
        h = modulate(x, norm_ffn_g[l], sh2, sc2)
        ff = (jax.nn.silu(h @ w_gate[l]) * (h @ w_up[l])) @ w_down[l]
        x = x + g2[:, None, :] * ff
    return rmsnorm(x, final_g)


def setup_inputs(seed: int = 0) -> dict:
    key = jax.random.key(seed)
    ks = jax.random.split(key, 24)
    nrm = lambda k, shape, s: jax.random.normal(k, shape, jnp.float32) * s
    D = D_MODEL
    return {
        "x_prompt": nrm(ks[0], (BATCH, SEQ, D), 1.0),
        "x_sample": nrm(ks[1], (DEC_BATCH, DEC_SEQ, D), 1.0),
        "c_prompt": nrm(ks[2], (BATCH, D), 1.0),
        "c_sample": nrm(ks[3], (DEC_BATCH, D), 1.0),
        "rel_bias": nrm(ks[4], (N_BUCKETS, N_BIAS_HEADS), 0.5),
        "ada_w": nrm(ks[5], (DEPTH, D, 6 * D), 0.5 * D ** -0.5),
        "ada_b": nrm(ks[6], (DEPTH, 6 * D), 0.02),
        "norm_mix_g": 1.0 + nrm(ks[7], (DEPTH, D), 0.02),
        "norm_ffn_g": 1.0 + nrm(ks[8], (DEPTH, D), 0.02),
        "w_in": nrm(ks[9], (DEPTH, D, IN_WIDTH), D ** -0.5),
        "w_out": nrm(ks[10], (DEPTH, MIX_WIDTH, D), MIX_WIDTH ** -0.5),
        "lambda_q1": nrm(ks[11], (DEPTH, HEAD_DIM), 0.1),
        "lambda_k1": nrm(ks[12], (DEPTH, HEAD_DIM), 0.1),
        "lambda_q2": nrm(ks[13], (DEPTH, HEAD_DIM), 0.1),
        "lambda_k2": nrm(ks[14], (DEPTH, HEAD_DIM), 0.1),
        "subln_g": 1.0 + nrm(ks[15], (DEPTH, 2 * HEAD_DIM), 0.02),
        "w_gate": nrm(ks[16], (DEPTH, D, D_FF), D ** -0.5),
        "w_up": nrm(ks[17], (DEPTH, D, D_FF), D ** -0.5),
        "w_down": nrm(ks[18], (DEPTH, D_FF, D), D_FF ** -0.5),
        "final_g": 1.0 + nrm(ks[19], (D,), 0.02),
    }


def reference(x_prompt, x_sample, c_prompt, c_sample, rel_bias, ada_w, ada_b,
              norm_mix_g, norm_ffn_g, w_in, w_out, lambda_q1, lambda_k1, lambda_q2,
              lambda_k2, subln_g, w_gate, w_up, w_down, final_g):
    y_prompt = trunk(x_prompt, c_prompt, rel_bias, ada_w, ada_b, norm_mix_g, norm_ffn_g,
                     w_in, w_out, lambda_q1, lambda_k1, lambda_q2, lambda_k2, subln_g,
                     w_gate, w_up, w_down, final_g)
    y_sample = trunk(x_sample, c_sample, rel_bias, ada_w, ada_b, norm_mix_g, norm_ffn_g,
                     w_in, w_out, lambda_q1, lambda_k1, lambda_q2, lambda_k2, subln_g,
                     w_gate, w_up, w_down, final_g)
    return (y_prompt, y_sample)
```

```python
import functools
import math

import jax
import jax.numpy as jnp
import numpy as np
from jax import lax
from jax.experimental import pallas as pl
from jax.experimental.pallas import tpu as pltpu

D_MODEL = 1024
HEAD_DIM = 64
N_HEADS_A = 8
N_HEADS_B = 4
WIDTH = 512
D_FF = 2816
DEPTH = 2
PATTERNS = ((128, 1), (512, 4), (2048, 16))
BAND = 64
N_BUCKETS = 32
MAX_DISTANCE = 1024
EPS = 1e-6
NEG = -1e30
LOG2E = 1.4426950408889634
Q_SCALE = LOG2E * HEAD_DIM ** -0.5

LANES = 128
BAND_TQ = 128
BAND_TK = BAND_TQ + 2 * BAND
DIFF_TQ = 256
DIFF_TK = 512
DIFF_TILE_STEP = 256
DIFF_TILE_LO = -5
DIFF_TILE_HI = 4
TOKEN_TILE = 256
FF_CHUNK = 256
VMEM_LIMIT = 56 * 1024 * 1024

F32 = jnp.float32
BF16 = jnp.bfloat16


def _params(*sem):
    return pltpu.CompilerParams(dimension_semantics=sem, vmem_limit_bytes=VMEM_LIMIT)


def _t5_bucket_np(rel):
    nb = N_BUCKETS // 2
    max_exact = nb // 2
    ret = (rel > 0).astype(np.int32) * nb
    n = np.abs(rel)
    nf = np.maximum(n, 1).astype(np.float32)
    large = max_exact + (np.log(nf / np.float32(max_exact))
                         / np.float32(math.log(MAX_DISTANCE / max_exact))
                         * np.float32(nb - max_exact)).astype(np.int32)
    large = np.minimum(large, nb - 1)
    return ret + np.where(n < max_exact, n, large)


def _toeplitz_offsets(base, tq, tk):
    t = np.arange(tq + tk)
    return base + np.where(t < tk, t, t - (tq + tk))


def _toeplitz(vecs, tq, tk):
    p = tq + tk
    lead = vecs.shape[:-1]
    flat = jnp.tile(vecs, (1,) * len(lead) + (tq,))
    return flat[..., :tq * (p - 1)].reshape(lead + (tq, p - 1))[..., :tk]


def _band_bias_tiles(table_a):
    tiles = []
    for _, d in PATTERNS:
        delta = _toeplitz_offsets(-BAND, BAND_TQ, BAND_TK)
        idx = _t5_bucket_np(delta * d)
        vec = jnp.where(jnp.asarray(np.abs(delta) <= BAND)[None, :],
                        table_a.T[:, idx] * LOG2E, NEG)
        tiles.append(_toeplitz(vec, BAND_TQ, BAND_TK))
    return jnp.stack(tiles)


def _diff_bias_tiles(table_b):
    vecs = []
    for t in range(DIFF_TILE_LO, DIFF_TILE_HI + 1):
        delta = _toeplitz_offsets(t * DIFF_TILE_STEP, DIFF_TQ, DIFF_TK)
        vecs.append(table_b.T[:, _t5_bucket_np(delta)] * LOG2E)
    return _toeplitz(jnp.stack(vecs, axis=1), DIFF_TQ, DIFF_TK)


def _mod_kernel(c_ref, w_ref, b_ref, o_ref):
    c = c_ref[...]
    s = c * jax.nn.sigmoid(c)
    o_ref[0] = jnp.dot(s, w_ref[0], precision=lax.Precision.HIGHEST,
                       preferred_element_type=F32) + b_ref[0]


def _modulation(c_all, ada_w, ada_b):
    rows = c_all.shape[0]
    n_col = 6 * D_MODEL // D_MODEL
    return pl.pallas_call(
        _mod_kernel,
        grid=(DEPTH, n_col),
        in_specs=[
            pl.BlockSpec((rows, D_MODEL), lambda l, j: (0, 0)),
            pl.BlockSpec((1, D_MODEL, D_MODEL), lambda l, j: (l, 0, j)),
            pl.BlockSpec((1, 1, D_MODEL), lambda l, j: (l, 0, j)),
        ],
        out_specs=pl.BlockSpec((1, rows, D_MODEL), lambda l, j: (l, 0, j)),
        out_shape=jax.ShapeDtypeStruct((DEPTH, rows, 6 * D_MODEL), F32),
        compiler_params=_params("arbitrary", "arbitrary"),
        name="adaln_modulation",
    )(c_all, ada_w, ada_b.reshape(DEPTH, 1, 6 * D_MODEL))


def _rms(x, g):
    return x * lax.rsqrt(jnp.mean(x * x, axis=-1, keepdims=True) + EPS) * g


def _inproj_kernel(x_ref, mod_ref, g_ref, w_ref, *outs):
    x = x_ref[0]
    h = _rms(x, g_ref[...]) * (1.0 + mod_ref[0, 1:2, :]) + mod_ref[0, 0:1, :]
    h = h.astype(BF16)
    for i, o_ref in enumerate(outs):
        p = jnp.dot(h, w_ref[:, i * WIDTH:(i + 1) * WIDTH], preferred_element_type=F32)
        if i % 3 == 0:
            p = p * Q_SCALE
        o_ref[0] = p.astype(BF16)


def _inproj(x, mod, g, w_in):
    b, l, _ = x.shape
    tm = TOKEN_TILE
    out = jax.ShapeDtypeStruct((b, l, WIDTH), BF16)
    return pl.pallas_call(
        _inproj_kernel,
        grid=(b, l // tm),
        in_specs=[
            pl.BlockSpec((1, tm, D_MODEL), lambda i, j: (i, j, 0)),
            pl.BlockSpec((1, 6, D_MODEL), lambda i, j: (i, 0, 0)),
            pl.BlockSpec((1, D_MODEL), lambda i, j: (0, 0)),
            pl.BlockSpec((D_MODEL, 6 * WIDTH), lambda i, j: (0, 0)),
        ],
        out_specs=[pl.BlockSpec((1, tm, WIDTH), lambda i, j: (i, j, 0))] * 6,
        out_shape=[out] * 6,
        compiler_params=_params("arbitrary", "arbitrary"),
        name="inproj",
    )(x, mod, g, w_in)


def _lane_half_mask(half):
    lane = lax.broadcasted_iota(jnp.int32, (1, LANES), 1)
    return (lane // HEAD_DIM == half).astype(F32)


def _band_kernel(q_ref, kp_ref, kc_ref, kn_ref, vp_ref, vc_ref, vn_ref, bias_ref,
                 o_ref, lse_ref):
    j = pl.program_id(1)
    last = pl.num_programs(1) - 1
    col = lax.broadcasted_iota(jnp.int32, (1, BAND_TK), 1)
    edge = (jnp.where(col < BAND, jnp.where(j == 0, NEG, 0.0), 0.0)
            + jnp.where(col >= BAND + BAND_TQ, jnp.where(j == last, NEG, 0.0), 0.0))
    lane = lax.broadcasted_iota(jnp.int32, (1, LANES), 1)
    for pair in range(N_HEADS_A // 2):
        sl = slice(pair * LANES, (pair + 1) * LANES)
        q = q_ref[0, :, sl].astype(F32)
        k = jnp.concatenate([kp_ref[0, :, sl], kc_ref[0, :, sl], kn_ref[0, :, sl]], axis=0)
        v = jnp.concatenate([vp_ref[0, :, sl], vc_ref[0, :, sl], vn_ref[0, :, sl]], axis=0)
        o_pair = None
        lse_pair = None
        for half in range(2):
            qm = (q * _lane_half_mask(half)).astype(BF16)
            s = lax.dot_general(qm, k, (((1,), (1,)), ((), ())), preferred_element_type=F32)
            s = s + bias_ref[2 * pair + half] + edge
            m = jnp.max(s, axis=-1, keepdims=True)
            p = jnp.exp2(s - m)
            den = jnp.sum(p, axis=-1, keepdims=True)
            acc = jnp.dot(p.astype(BF16), v, preferred_element_type=F32)
            o_h = acc / den
            lse_h = jnp.broadcast_to(m + jnp.log(den) * LOG2E, (BAND_TQ, LANES))
            if half == 0:
                o_pair, lse_pair = o_h, lse_h
            else:
                o_pair = jnp.where(lane < HEAD_DIM, o_pair, o_h)
                lse_pair = jnp.where(lane < HEAD_DIM, lse_pair, lse_h)
        o_ref[0, :, sl] = o_pair
        lse_ref[0, :, sl] = lse_pair


def _band_attention(q, k, v, bias):
    s, n, _ = q.shape
    nq = n // BAND_TQ
    nk = n // BAND
    per = BAND_TQ // BAND
    cur = pl.BlockSpec((1, BAND_TQ, WIDTH), lambda i, j: (i, j, 0))
    prev = pl.BlockSpec((1, BAND, WIDTH), lambda i, j: (i, jnp.maximum(j * per - 1, 0), 0))
    nxt = pl.BlockSpec((1, BAND, WIDTH), lambda i, j: (i, jnp.minimum((j + 1) * per, nk - 1), 0))
    out = jax.ShapeDtypeStruct((s, n, WIDTH), F32)
    return pl.pallas_call(
        _band_kernel,
        grid=(s, nq),
        in_specs=[cur, prev, cur, nxt, prev, cur, nxt,
                  pl.BlockSpec((N_HEADS_A, BAND_TQ, BAND_TK), lambda i, j: (0, 0, 0))],
        out_specs=[cur, cur],
        out_shape=[out, out],
        compiler_params=_params("arbitrary", "arbitrary"),
        name="band_attention",
    )(q, k, k, k, v, v, v, bias)


def _to_sub(t, d):
    b, l, w = t.shape
    return t.reshape(b, l // d, d, w).transpose(0, 2, 1, 3).reshape(b * d, l // d, w)


def _from_sub(t, b, d):
    s, n, w = t.shape
    return t.reshape(b, d, n, w).transpose(0, 2, 1, 3).reshape(b, n * d, w)


def _diff_kernel(lam_init, q0_ref, q1_ref, k0_ref, k1_ref, v_ref, bias_ref,
                 lq1_ref, lk1_ref, lq2_ref, lk2_ref, g_ref, o_ref,
                 m0_ref, l0_ref, a0_ref, m1_ref, l1_ref, a1_ref):
    head = pl.program_id(1)
    qi = pl.program_id(2)
    n_kv = k0_ref.shape[1] // DIFF_TK
    lane = lax.broadcasted_iota(jnp.int32, (1, LANES), 1)
    mask = (lane // HEAD_DIM == head % 2).astype(F32)
    q0 = (q0_ref[0].astype(F32) * mask).astype(BF16)
    q1 = (q1_ref[0].astype(F32) * mask).astype(BF16)
    for m_ref, l_ref, a_ref in ((m0_ref, l0_ref, a0_ref), (m1_ref, l1_ref, a1_ref)):
        m_ref[...] = jnp.full(m_ref.shape, NEG, F32)
        l_ref[...] = jnp.zeros(l_ref.shape, F32)
        a_ref[...] = jnp.zeros(a_ref.shape, F32)

    def step(kk, carry):
        start = pl.multiple_of(kk * DIFF_TK, DIFF_TK)
        off = (kk * DIFF_TK - qi * DIFF_TQ) // DIFF_TILE_STEP
        bias = bias_ref[0, jnp.clip(off, DIFF_TILE_LO, DIFF_TILE_HI) - DIFF_TILE_LO]
        v = v_ref[0, pl.ds(start, DIFF_TK), :]
        for q, k_ref, m_ref, l_ref, a_ref in ((q0, k0_ref, m0_ref, l0_ref, a0_ref),
                                               (q1, k1_ref, m1_ref, l1_ref, a1_ref)):
            k = k_ref[0, pl.ds(start, DIFF_TK), :]
            s = lax.dot_general(q, k, (((1,), (1,)), ((), ())), preferred_element_type=F32)
            s = s + bias
            m_old = m_ref[...]
            m_new = jnp.maximum(m_old, jnp.max(s, axis=-1, keepdims=True))
            alpha = jnp.exp2(m_old - m_new)
            p = jnp.exp2(s - m_new)
            l_ref[...] = alpha * l_ref[...] + jnp.sum(p, axis=-1, keepdims=True)
            a_ref[...] = alpha * a_ref[...] + jnp.dot(p.astype(BF16), v,
                                                      preferred_element_type=F32)
            m_ref[...] = m_new
        return carry

    lax.fori_loop(0, n_kv, step, 0)

    lam = (jnp.exp(jnp.sum(lq1_ref[...] * lk1_ref[...], axis=-1, keepdims=True))
           - jnp.exp(jnp.sum(lq2_ref[...] * lk2_ref[...], axis=-1, keepdims=True))
           + lam_init)
    o = a0_ref[...] / l0_ref[...] - lam * (a1_ref[...] / l1_ref[...])
    o_ref[0] = (_rms(o, g_ref[...]) * (1.0 - lam_init)).astype(BF16)


def _diff_attention(qb, kb, vb, bias, lq1, lk1, lq2, lk2, g, lam_init):
    b, l, _ = qb.shape
    nq = l // DIFF_TQ
    n_tiles = bias.shape[1]
    pairs = N_HEADS_B // 2
    qspec = lambda m: pl.BlockSpec((1, DIFF_TQ, LANES), lambda i, h, j: (i, j, m * pairs + h // 2))
    kspec = lambda m: pl.BlockSpec((1, l, LANES), lambda i, h, j: (i, 0, m * pairs + h // 2))
    small = pl.BlockSpec((1, HEAD_DIM), lambda i, h, j: (0, 0))
    stat = pltpu.VMEM((DIFF_TQ, 1), F32)
    acc = pltpu.VMEM((DIFF_TQ, LANES), F32)
    return pl.pallas_call(
        functools.partial(_diff_kernel, lam_init),
        grid=(b, N_HEADS_B, nq),
        in_specs=[
            qspec(0), qspec(1), kspec(0), kspec(1),
            pl.BlockSpec((1, l, LANES), lambda i, h, j: (i, 0, h)),
            pl.BlockSpec((1, n_tiles, DIFF_TQ, DIFF_TK), lambda i, h, j: (h, 0, 0, 0)),
            small, small, small, small,
            pl.BlockSpec((1, 2 * HEAD_DIM), lambda i, h, j: (0, 0)),
        ],
        out_specs=pl.BlockSpec((1, DIFF_TQ, LANES), lambda i, h, j: (i, j, h)),
        out_shape=jax.ShapeDtypeStruct((b, l, WIDTH), BF16),
        scratch_shapes=[stat, stat, acc, stat, stat, acc],
        compiler_params=_params("arbitrary", "arbitrary", "arbitrary"),
        name="diff_attention",
    )(qb, qb, kb, kb, vb, bias, lq1, lk1, lq2, lk2, g)


def _outproj_kernel(x_ref, o1_ref, o2_ref, o3_ref, l1_ref, l2_ref, l3_ref, ob_ref,
                    mod_ref, w_ref, y_ref):
    l1, l2, l3 = l1_ref[0], l2_ref[0], l3_ref[0]
    mx = jnp.maximum(jnp.maximum(l1, l2), l3)
    e1, e2, e3 = jnp.exp2(l1 - mx), jnp.exp2(l2 - mx), jnp.exp2(l3 - mx)
    oa = (e1 * o1_ref[0] + e2 * o2_ref[0] + e3 * o3_ref[0]) / (e1 + e2 + e3)
    mix = (jnp.dot(oa.astype(BF16), w_ref[0:WIDTH, :], preferred_element_type=F32)
           + jnp.dot(ob_ref[0], w_ref[WIDTH:2 * WIDTH, :], preferred_element_type=F32))
    y_ref[0] = x_ref[0] + mod_ref[0, 2:3, :] * mix


def _outproj(x, o1, o2, o3, l1, l2, l3, ob, mod, w_out):
    b, l, _ = x.shape
    tm = TOKEN_TILE
    xs = pl.BlockSpec((1, tm, D_MODEL), lambda i, j: (i, j, 0))
    hs = pl.BlockSpec((1, tm, WIDTH), lambda i, j: (i, j, 0))
    return pl.pallas_call(
        _outproj_kernel,
        grid=(b, l // tm),
        in_specs=[xs, hs, hs, hs, hs, hs, hs, hs,
                  pl.BlockSpec((1, 6, D_MODEL), lambda i, j: (i, 0, 0)),
                  pl.BlockSpec((2 * WIDTH, D_MODEL), lambda i, j: (0, 0))],
        out_specs=xs,
        out_shape=jax.ShapeDtypeStruct(x.shape, F32),
        compiler_params=_params("arbitrary", "arbitrary"),
        name="outproj",
    )(x, o1, o2, o3, l1, l2, l3, ob, mod, w_out)


def _ffn_kernel(final, x_ref, mod_ref, g_ref, wg_ref, wu_ref, wd_ref, fg_ref, y_ref, act_ref):
    x = x_ref[0]
    h = _rms(x, g_ref[...]) * (1.0 + mod_ref[0, 4:5, :]) + mod_ref[0, 3:4, :]
    h = h.astype(BF16)
    for c in range(D_FF // FF_CHUNK):
        sl = slice(c * FF_CHUNK, (c + 1) * FF_CHUNK)
        gate = jnp.dot(h, wg_ref[:, sl], preferred_element_type=F32)
        up = jnp.dot(h, wu_ref[:, sl], preferred_element_type=F32)
        act_ref[:, sl] = (gate * jax.nn.sigmoid(gate) * up).astype(BF16)
    ff = jnp.dot(act_ref[...], wd_ref[...], preferred_element_type=F32)
    y = x + mod_ref[0, 5:6, :] * ff
    if final:
        y = _rms(y, fg_ref[...])
    y_ref[0] = y


def _ffn(x, mod, g, w_gate, w_up, w_down, final_g, final):
    b, l, _ = x.shape
    tm = TOKEN_TILE
    xs = pl.BlockSpec((1, tm, D_MODEL), lambda i, j: (i, j, 0))
    vec = pl.BlockSpec((1, D_MODEL), lambda i, j: (0, 0))
    once = pl.Buffered(1)
    return pl.pallas_call(
        functools.partial(_ffn_kernel, final),
        grid=(b, l // tm),
        in_specs=[xs, pl.BlockSpec((1, 6, D_MODEL), lambda i, j: (i, 0, 0)), vec,
                  pl.BlockSpec((D_MODEL, D_FF), lambda i, j: (0, 0), pipeline_mode=once),
                  pl.BlockSpec((D_MODEL, D_FF), lambda i, j: (0, 0), pipeline_mode=once),
                  pl.BlockSpec((D_FF, D_MODEL), lambda i, j: (0, 0), pipeline_mode=once),
                  vec],
        out_specs=xs,
        out_shape=jax.ShapeDtypeStruct(x.shape, F32),
        scratch_shapes=[pltpu.VMEM((tm, D_FF), BF16)],
        compiler_params=_params("arbitrary", "arbitrary"),
        name="ffn",
    )(x, mod, g, w_gate, w_up, w_down, final_g)


def _trunk(x, mods, band_bias, diff_bias, p):
    b = x.shape[0]
    for l in range(DEPTH):
        mod = mods[l]
        qa, ka, va, qb, kb, vb = _inproj(x, mod, p["norm_mix_g"][l:l + 1], p["w_in"][l])
        outs = []
        for i, (_, d) in enumerate(PATTERNS):
            if d == 1:
                o, lse = _band_attention(qa, ka, va, band_bias[i])
            else:
                o, lse = _band_attention(_to_sub(qa, d), _to_sub(ka, d), _to_sub(va, d),
                                         band_bias[i])
                o, lse = _from_sub(o, b, d), _from_sub(lse, b, d)
            outs.append((o, lse))
        lam_init = 0.8 - 0.6 * math.exp(-0.3 * l)
        ob = _diff_attention(qb, kb, vb, diff_bias,
                             p["lambda_q1"][l:l + 1], p["lambda_k1"][l:l + 1],
                             p["lambda_q2"][l:l + 1], p["lambda_k2"][l:l + 1],
                             p["subln_g"][l:l + 1], lam_init)
        x = _outproj(x, outs[0][0], outs[1][0], outs[2][0], outs[0][1], outs[1][1], outs[2][1],
                     ob, mod, p["w_out"][l])
        x = _ffn(x, mod, p["norm_ffn_g"][l:l + 1], p["w_gate"][l], p["w_up"][l], p["w_down"][l],
                 p["final_g"], final=(l == DEPTH - 1))
    return x


def kernel(x_prompt, x_sample, c_prompt, c_sample, rel_bias, ada_w, ada_b, norm_mix_g,
           norm_ffn_g, w_in, w_out, lambda_q1, lambda_k1, lambda_q2, lambda_k2, subln_g,
           w_gate, w_up, w_down, final_g):
    nb_p, nb_s = c_prompt.shape[0], c_sample.shape[0]
    rows = -(-(nb_p + nb_s) // 8) * 8
    c_all = jnp.zeros((rows, D_MODEL), F32).at[:nb_p].set(c_prompt).at[nb_p:nb_p + nb_s].set(c_sample)
    mods = _modulation(c_all, ada_w, ada_b).reshape(DEPTH, rows, 6, D_MODEL)
    band_bias = _band_bias_tiles(rel_bias[:, :N_HEADS_A])
    diff_bias = _diff_bias_tiles(rel_bias[:, N_HEADS_A:])
    p = dict(norm_mix_g=norm_mix_g, norm_ffn_g=norm_ffn_g, w_in=w_in.astype(BF16),
             w_out=w_out.astype(BF16), lambda_q1=lambda_q1, lambda_k1=lambda_k1,
             lambda_q2=lambda_q2, lambda_k2=lambda_k2, subln_g=subln_g,
             w_gate=w_gate.astype(BF16), w_up=w_up.astype(BF16), w_down=w_down.astype(BF16),
             final_g=final_g.reshape(1, D_MODEL))
    y_prompt = _trunk(x_prompt, mods[:, :nb_p], band_bias, diff_bias, p)
    y_sample = _trunk(x_sample, mods[:, nb_p:nb_p + nb_s], band_bias, diff_bias, p)
    return (y_prompt, y_sample)
```

```python
import functools
import math

import jax
import jax.numpy as jnp
import numpy as np
from jax import lax
from jax.experimental import pallas as pl
from jax.experimental.pallas import tpu as pltpu

D_MODEL = 1024
HEAD_DIM = 64
N_HEADS_A = 8
N_HEADS_B = 4
WIDTH = 512
D_FF = 2816
DEPTH = 2
PATTERNS = ((128, 1), (512, 4), (2048, 16))
BAND = 64
N_BUCKETS = 32
MAX_DISTANCE = 1024
EPS = 1e-6
NEG = -1e30
LOG2E = 1.4426950408889634
Q_SCALE = LOG2E * HEAD_DIM ** -0.5

LANES = 128
BAND_TQ = 128
BAND_TK = BAND_TQ + 2 * BAND
DIFF_TQ = 256
DIFF_TK = 512
DIFF_TILE_LO = -5
DIFF_TILE_HI = 4
TOKEN_TILE = 256
FF_CHUNK = 256
VMEM_LIMIT = 56 * 1024 * 1024

F32 = jnp.float32
BF16 = jnp.bfloat16


def _params(*sem):
    return pltpu.CompilerParams(dimension_semantics=sem, vmem_limit_bytes=VMEM_LIMIT)


def _t5_bucket_np(rel):
    nb = N_BUCKETS // 2
    max_exact = nb // 2
    ret = (rel > 0).astype(np.int32) * nb
    n = np.abs(rel)
    nf = np.maximum(n, 1).astype(np.float32)
    large = max_exact + (np.log(nf / np.float32(max_exact))
                         / np.float32(math.log(MAX_DISTANCE / max_exact))
                         * np.float32(nb - max_exact)).astype(np.int32)
    large = np.minimum(large, nb - 1)
    return ret + np.where(n < max_exact, n, large)


def _toeplitz_offsets(rows, cols):
    t = np.arange(rows + cols)
    return np.where(t < cols, t, t - (rows + cols))


def _toeplitz(vecs, tq, tk):
    p = tq + tk
    lead = vecs.shape[:-1]
    flat = jnp.tile(vecs, (1,) * len(lead) + (tq,))
    return flat[..., :tq * (p - 1)].reshape(lead + (tq, p - 1))[..., :tk]


def _band_bias_tiles(table_a):
    tiles = []
    for _, d in PATTERNS:
        delta = _toeplitz_offsets(BAND_TQ, BAND_TK) - BAND
        idx = _t5_bucket_np(delta * d)
        vec = jnp.where(jnp.asarray(np.abs(delta) <= BAND)[None, :],
                        table_a.T[:, idx] * LOG2E, NEG)
        tiles.append(_toeplitz(vec, BAND_TQ, BAND_TK))
    return jnp.stack(tiles)


def _diff_bias_tiles(table_b):
    vecs = []
    for t in range(DIFF_TILE_LO, DIFF_TILE_HI + 1):
        delta = t * DIFF_TQ - _toeplitz_offsets(DIFF_TK, DIFF_TQ)
        vecs.append(table_b.T[:, _t5_bucket_np(delta)] * LOG2E)
    return _toeplitz(jnp.stack(vecs, axis=1), DIFF_TK, DIFF_TQ)


def _mod_kernel(c_ref, w_ref, b_ref, o_ref):
    c = c_ref[...]
    s = c * jax.nn.sigmoid(c)
    o_ref[0] = jnp.dot(s, w_ref[0], precision=lax.Precision.HIGHEST,
                       preferred_element_type=F32) + b_ref[0]


def _modulation(c_all, ada_w, ada_b):
    rows = c_all.shape[0]
    n_col = 6 * D_MODEL // D_MODEL
    return pl.pallas_call(
        _mod_kernel,
        grid=(DEPTH, n_col),
        in_specs=[
            pl.BlockSpec((rows, D_MODEL), lambda l, j: (0, 0)),
            pl.BlockSpec((1, D_MODEL, D_MODEL), lambda l, j: (l, 0, j)),
            pl.BlockSpec((1, 1, D_MODEL), lambda l, j: (l, 0, j)),
        ],
        out_specs=pl.BlockSpec((1, rows, D_MODEL), lambda l, j: (l, 0, j)),
        out_shape=jax.ShapeDtypeStruct((DEPTH, rows, 6 * D_MODEL), F32),
        compiler_params=_params("arbitrary", "arbitrary"),
        name="adaln_modulation",
    )(c_all, ada_w, ada_b.reshape(DEPTH, 1, 6 * D_MODEL))


def _rms(x, g):
    return x * lax.rsqrt(jnp.mean(x * x, axis=-1, keepdims=True) + EPS) * g


def _inproj_kernel(x_ref, mod_ref, g_ref, w_ref, *outs):
    x = x_ref[0]
    h = _rms(x, g_ref[...]) * (1.0 + mod_ref[0, 1:2, :]) + mod_ref[0, 0:1, :]
    h = h.astype(BF16)
    for i, o_ref in enumerate(outs):
        p = jnp.dot(h, w_ref[:, i * WIDTH:(i + 1) * WIDTH], preferred_element_type=F32)
        if i % 3 == 0:
            p = p * Q_SCALE
        o_ref[0] = p.astype(BF16)


def _inproj(x, mod, g, w_in):
    b, l, _ = x.shape
    tm = TOKEN_TILE
    out = jax.ShapeDtypeStruct((b, l, WIDTH), BF16)
    return pl.pallas_call(
        _inproj_kernel,
        grid=(b, l // tm),
        in_specs=[
            pl.BlockSpec((1, tm, D_MODEL), lambda i, j: (i, j, 0)),
            pl.BlockSpec((1, 6, D_MODEL), lambda i, j: (i, 0, 0)),
            pl.BlockSpec((1, D_MODEL), lambda i, j: (0, 0)),
            pl.BlockSpec((D_MODEL, 6 * WIDTH), lambda i, j: (0, 0)),
        ],
        out_specs=[pl.BlockSpec((1, tm, WIDTH), lambda i, j: (i, j, 0))] * 6,
        out_shape=[out] * 6,
        compiler_params=_params("arbitrary", "arbitrary"),
        name="inproj",
    )(x, mod, g, w_in)


def _lane_half_mask(half):
    lane = lax.broadcasted_iota(jnp.int32, (1, LANES), 1)
    return (lane // HEAD_DIM == half).astype(F32)


def _band_kernel(q_ref, kp_ref, kc_ref, kn_ref, vp_ref, vc_ref, vn_ref, bias_ref,
                 o_ref, lse_ref):
    j = pl.program_id(1)
    last = pl.num_programs(1) - 1
    col = lax.broadcasted_iota(jnp.int32, (1, BAND_TK), 1)
    edge = (jnp.where(col < BAND, jnp.where(j == 0, NEG, 0.0), 0.0)
            + jnp.where(col >= BAND + BAND_TQ, jnp.where(j == last, NEG, 0.0), 0.0))
    lane = lax.broadcasted_iota(jnp.int32, (1, LANES), 1)
    for pair in range(N_HEADS_A // 2):
        sl = slice(pair * LANES, (pair + 1) * LANES)
        q = q_ref[0, :, sl].astype(F32)
        k = jnp.concatenate([kp_ref[0, :, sl], kc_ref[0, :, sl], kn_ref[0, :, sl]], axis=0)
        v = jnp.concatenate([vp_ref[0, :, sl], vc_ref[0, :, sl], vn_ref[0, :, sl]], axis=0)
        o_pair = None
        lse_pair = None
        for half in range(2):
            qm = (q * _lane_half_mask(half)).astype(BF16)
            s = lax.dot_general(qm, k, (((1,), (1,)), ((), ())), preferred_element_type=F32)
            s = s + bias_ref[2 * pair + half] + edge
            m = jnp.max(s, axis=-1, keepdims=True)
            p = jnp.exp2(s - m)
            den = jnp.sum(p, axis=-1, keepdims=True)
            acc = jnp.dot(p.astype(BF16), v, preferred_element_type=F32)
            o_h = acc / den
            lse_h = jnp.broadcast_to(m + jnp.log(den) * LOG2E, (BAND_TQ, LANES))
            if half == 0:
                o_pair, lse_pair = o_h, lse_h
            else:
                o_pair = jnp.where(lane < HEAD_DIM, o_pair, o_h)
                lse_pair = jnp.where(lane < HEAD_DIM, lse_pair, lse_h)
        o_ref[0, :, sl] = o_pair
        lse_ref[0, :, sl] = lse_pair


def _band_attention(q, k, v, bias):
    s, n, _ = q.shape
    nq = n // BAND_TQ
    nk = n // BAND
    per = BAND_TQ // BAND
    cur = pl.BlockSpec((1, BAND_TQ, WIDTH), lambda i, j: (i, j, 0))
    prev = pl.BlockSpec((1, BAND, WIDTH), lambda i, j: (i, jnp.maximum(j * per - 1, 0), 0))
    nxt = pl.BlockSpec((1, BAND, WIDTH), lambda i, j: (i, jnp.minimum((j + 1) * per, nk - 1), 0))
    out = jax.ShapeDtypeStruct((s, n, WIDTH), F32)
    return pl.pallas_call(
        _band_kernel,
        grid=(s, nq),
        in_specs=[cur, prev, cur, nxt, prev, cur, nxt,
                  pl.BlockSpec((N_HEADS_A, BAND_TQ, BAND_TK), lambda i, j: (0, 0, 0))],
        out_specs=[cur, cur],
        out_shape=[out, out],
        compiler_params=_params("arbitrary", "arbitrary"),
        name="band_attention",
    )(q, k, k, k, v, v, v, bias)


def _to_sub(t, d):
    b, l, w = t.shape
    return t.reshape(b, l // d, d, w).transpose(0, 2, 1, 3).reshape(b * d, l // d, w)


def _from_sub(t, b, d):
    s, n, w = t.shape
    return t.reshape(b, d, n, w).transpose(0, 2, 1, 3).reshape(b, n * d, w)


def _diff_kernel(lam_init, q0_ref, q1_ref, k0_ref, k1_ref, vt_ref, bias_ref,
                 lq1_ref, lk1_ref, lq2_ref, lk2_ref, g_ref, o_ref, a0_ref, a1_ref, s_ref):
    head = pl.program_id(1)
    qi = pl.program_id(2)
    n_kv = k0_ref.shape[1] // DIFF_TK
    lane = lax.broadcasted_iota(jnp.int32, (1, LANES), 1)
    mask = (lane // HEAD_DIM == head % 2).astype(F32)
    q0t = (q0_ref[0].astype(F32) * mask).T.astype(BF16)
    q1t = (q1_ref[0].astype(F32) * mask).T.astype(BF16)
    a0_ref[...] = jnp.zeros(a0_ref.shape, F32)
    a1_ref[...] = jnp.zeros(a1_ref.shape, F32)
    maps = ((q0t, k0_ref, a0_ref), (q1t, k1_ref, a1_ref))

    def scores(kk, slot):
        start = pl.multiple_of(kk * DIFF_TK, DIFF_TK)
        off = kk * (DIFF_TK // DIFF_TQ) - qi
        bias = bias_ref[0, jnp.clip(off, DIFF_TILE_LO, DIFF_TILE_HI) - DIFF_TILE_LO]
        cmax = []
        for m, (qt, k_ref, _) in enumerate(maps):
            s = jnp.dot(k_ref[0, pl.ds(start, DIFF_TK), :], qt, preferred_element_type=F32) + bias
            s_ref[slot, m] = s
            cmax.append(jnp.max(s, axis=0, keepdims=True))
        return tuple(cmax)

    def consume(kk, slot, cmax, stats):
        vt = vt_ref[0, 0, kk]
        out = []
        for m, (_, _, a_ref) in enumerate(maps):
            m_old, l_old = stats[m]
            m_new = jnp.maximum(m_old, cmax[m])
            alpha = jnp.exp2(m_old - m_new)
            p = jnp.exp2(s_ref[slot, m] - m_new)
            l_new = alpha * l_old + jnp.sum(p, axis=0, keepdims=True)
            a_ref[...] = alpha * a_ref[...] + jnp.dot(vt, p.astype(BF16),
                                                      preferred_element_type=F32)
            out.append((m_new, l_new))
        return tuple(out)

    def pair(j, carry):
        cmax, stats = carry
        nxt = scores(2 * j + 1, 1)
        stats = consume(2 * j, 0, cmax, stats)
        cmax = scores(2 * j + 2, 0)
        stats = consume(2 * j + 1, 1, nxt, stats)
        return cmax, stats

    stat = (jnp.full((1, DIFF_TQ), NEG, F32), jnp.zeros((1, DIFF_TQ), F32))
    cmax, stats = lax.fori_loop(0, n_kv // 2 - 1, pair, (scores(0, 0), (stat, stat)))
    nxt = scores(n_kv - 1, 1)
    stats = consume(n_kv - 2, 0, cmax, stats)
    (_, l0), (_, l1) = consume(n_kv - 1, 1, nxt, stats)

    lam = (jnp.exp(jnp.sum(lq1_ref[...] * lk1_ref[...], axis=-1, keepdims=True))
           - jnp.exp(jnp.sum(lq2_ref[...] * lk2_ref[...], axis=-1, keepdims=True))
           + lam_init)
    o = (a0_ref[...] / l0 - lam * (a1_ref[...] / l1)).T
    o_ref[0] = (_rms(o, g_ref[...]) * (1.0 - lam_init)).astype(BF16)


def _diff_attention(qb, kb, vb, bias, lq1, lk1, lq2, lk2, g, lam_init):
    b, l, _ = qb.shape
    nq = l // DIFF_TQ
    nk = l // DIFF_TK
    n_tiles = bias.shape[1]
    pairs = N_HEADS_B // 2
    vt = vb.reshape(b, nk, DIFF_TK, N_HEADS_B, LANES).transpose(0, 3, 1, 4, 2)
    qspec = lambda m: pl.BlockSpec((1, DIFF_TQ, LANES), lambda i, h, j: (i, j, m * pairs + h // 2))
    kspec = lambda m: pl.BlockSpec((1, l, LANES), lambda i, h, j: (i, 0, m * pairs + h // 2))
    small = pl.BlockSpec((1, HEAD_DIM), lambda i, h, j: (0, 0))
    acc = pltpu.VMEM((LANES, DIFF_TQ), F32)
    return pl.pallas_call(
        functools.partial(_diff_kernel, lam_init),
        grid=(b, N_HEADS_B, nq),
        in_specs=[
            qspec(0), qspec(1), kspec(0), kspec(1),
            pl.BlockSpec((1, 1, nk, LANES, DIFF_TK), lambda i, h, j: (i, h, 0, 0, 0)),
            pl.BlockSpec((1, n_tiles, DIFF_TK, DIFF_TQ), lambda i, h, j: (h, 0, 0, 0)),
            small, small, small, small,
            pl.BlockSpec((1, 2 * HEAD_DIM), lambda i, h, j: (0, 0)),
        ],
        out_specs=pl.BlockSpec((1, DIFF_TQ, LANES), lambda i, h, j: (i, j, h)),
        out_shape=jax.ShapeDtypeStruct((b, l, WIDTH), BF16),
        scratch_shapes=[acc, acc, pltpu.VMEM((2, 2, DIFF_TK, DIFF_TQ), F32)],
        compiler_params=_params("arbitrary", "arbitrary", "arbitrary"),
        name="diff_attention",
    )(qb, qb, kb, kb, vt, bias, lq1, lk1, lq2, lk2, g)


def _outproj_kernel(x_ref, o1_ref, o2_ref, o3_ref, l1_ref, l2_ref, l3_ref, ob_ref,
                    mod_ref, w_ref, y_ref):
    l1, l2, l3 = l1_ref[0], l2_ref[0], l3_ref[0]
    mx = jnp.maximum(jnp.maximum(l1, l2), l3)
    e1, e2, e3 = jnp.exp2(l1 - mx), jnp.exp2(l2 - mx), jnp.exp2(l3 - mx)
    oa = (e1 * o1_ref[0] + e2 * o2_ref[0] + e3 * o3_ref[0]) / (e1 + e2 + e3)
    mix = (jnp.dot(oa.astype(BF16), w_ref[0:WIDTH, :], preferred_element_type=F32)
           + jnp.dot(ob_ref[0], w_ref[WIDTH:2 * WIDTH, :], preferred_element_type=F32))
    y_ref[0] = x_ref[0] + mod_ref[0, 2:3, :] * mix


def _outproj(x, o1, o2, o3, l1, l2, l3, ob, mod, w_out):
    b, l, _ = x.shape
    tm = TOKEN_TILE
    xs = pl.BlockSpec((1, tm, D_MODEL), lambda i, j: (i, j, 0))
    hs = pl.BlockSpec((1, tm, WIDTH), lambda i, j: (i, j, 0))
    return pl.pallas_call(
        _outproj_kernel,
        grid=(b, l // tm),
        in_specs=[xs, hs, hs, hs, hs, hs, hs, hs,
                  pl.BlockSpec((1, 6, D_MODEL), lambda i, j: (i, 0, 0)),
                  pl.BlockSpec((2 * WIDTH, D_MODEL), lambda i, j: (0, 0))],
        out_specs=xs,
        out_shape=jax.ShapeDtypeStruct(x.shape, F32),
        compiler_params=_params("arbitrary", "arbitrary"),
        name="outproj",
    )(x, o1, o2, o3, l1, l2, l3, ob, mod, w_out)


def _ffn_kernel(final, x_ref, mod_ref, g_ref, wg_ref, wu_ref, wd_ref, fg_ref, y_ref, act_ref):
    x = x_ref[0]
    h = _rms(x, g_ref[...]) * (1.0 + mod_ref[0, 4:5, :]) + mod_ref[0, 3:4, :]
    h = h.astype(BF16)
    for c in range(D_FF // FF_CHUNK):
        sl = slice(c * FF_CHUNK, (c + 1) * FF_CHUNK)
        gate = jnp.dot(h, wg_ref[:, sl], preferred_element_type=F32)
        up = jnp.dot(h, wu_ref[:, sl], preferred_element_type=F32)
        act_ref[:, sl] = (gate * jax.nn.sigmoid(gate) * up).astype(BF16)
    ff = jnp.dot(act_ref[...], wd_ref[...], preferred_element_type=F32)
    y = x + mod_ref[0, 5:6, :] * ff
    if final:
        y = _rms(y, fg_ref[...])
    y_ref[0] = y


def _ffn(x, mod, g, w_gate, w_up, w_down, final_g, final):
    b, l, _ = x.shape
    tm = TOKEN_TILE
    xs = pl.BlockSpec((1, tm, D_MODEL), lambda i, j: (i, j, 0))
    vec = pl.BlockSpec((1, D_MODEL), lambda i, j: (0, 0))
    once = pl.Buffered(1)
    return pl.pallas_call(
        functools.partial(_ffn_kernel, final),
        grid=(b, l // tm),
        in_specs=[xs, pl.BlockSpec((1, 6, D_MODEL), lambda i, j: (i, 0, 0)), vec,
                  pl.BlockSpec((D_MODEL, D_FF), lambda i, j: (0, 0), pipeline_mode=once),
                  pl.BlockSpec((D_MODEL, D_FF), lambda i, j: (0, 0), pipeline_mode=once),
                  pl.BlockSpec((D_FF, D_MODEL), lambda i, j: (0, 0), pipeline_mode=once),
                  vec],
        out_specs=xs,
        out_shape=jax.ShapeDtypeStruct(x.shape, F32),
        scratch_shapes=[pltpu.VMEM((tm, D_FF), BF16)],
        compiler_params=_params("arbitrary", "arbitrary"),
        name="ffn",
    )(x, mod, g, w_gate, w_up, w_down, final_g)


def _trunk(x, mods, band_bias, diff_bias, p):
    b = x.shape[0]
    for l in range(DEPTH):
        mod = mods[l]
        qa, ka, va, qb, kb, vb = _inproj(x, mod, p["norm_mix_g"][l:l + 1], p["w_in"][l])
        outs = []
        for i, (_, d) in enumerate(PATTERNS):
            if d == 1:
                o, lse = _band_attention(qa, ka, va, band_bias[i])
            else:
                o, lse = _band_attention(_to_sub(qa, d), _to_sub(ka, d), _to_sub(va, d),
                                         band_bias[i])
                o, lse = _from_sub(o, b, d), _from_sub(lse, b, d)
            outs.append((o, lse))
        lam_init = 0.8 - 0.6 * math.exp(-0.3 * l)
        ob = _diff_attention(qb, kb, vb, diff_bias,
                             p["lambda_q1"][l:l + 1], p["lambda_k1"][l:l + 1],
                             p["lambda_q2"][l:l + 1], p["lambda_k2"][l:l + 1],
                             p["subln_g"][l:l + 1], lam_init)
        x = _outproj(x, outs[0][0], outs[1][0], outs[2][0], outs[0][1], outs[1][1], outs[2][1],
                     ob, mod, p["w_out"][l])
        x = _ffn(x, mod, p["norm_ffn_g"][l:l + 1], p["w_gate"][l], p["w_up"][l], p["w_down"][l],
                 p["final_g"], final=(l == DEPTH - 1))
    return x


def kernel(x_prompt, x_sample, c_prompt, c_sample, rel_bias, ada_w, ada_b, norm_mix_g,
           norm_ffn_g, w_in, w_out, lambda_q1, lambda_k1, lambda_q2, lambda_k2, subln_g,
           w_gate, w_up, w_down, final_g):
    nb_p, nb_s = c_prompt.shape[0], c_sample.shape[0]
    rows = -(-(nb_p + nb_s) // 8) * 8
    c_all = jnp.zeros((rows, D_MODEL), F32).at[:nb_p].set(c_prompt).at[nb_p:nb_p + nb_s].set(c_sample)
    mods = _modulation(c_all, ada_w, ada_b).reshape(DEPTH, rows, 6, D_MODEL)
    band_bias = _band_bias_tiles(rel_bias[:, :N_HEADS_A])
    diff_bias = _diff_bias_tiles(rel_bias[:, N_HEADS_A:])
    p = dict(norm_mix_g=norm_mix_g, norm_ffn_g=norm_ffn_g, w_in=w_in.astype(BF16),
             w_out=w_out.astype(BF16), lambda_q1=lambda_q1, lambda_k1=lambda_k1,
             lambda_q2=lambda_q2, lambda_k2=lambda_k2, subln_g=subln_g,
             w_gate=w_gate.astype(BF16), w_up=w_up.astype(BF16), w_down=w_down.astype(BF16),
             final_g=final_g.reshape(1, D_MODEL))
    y_prompt = _trunk(x_prompt, mods[:, :nb_p], band_bias, diff_bias, p)
    y_sample = _trunk(x_sample, mods[:, nb_p:nb_p + nb_s], band_bias, diff_bias, p)
    return (y_prompt, y_sample)
```

```python
import functools
import math

import jax
import jax.numpy as jnp
import numpy as np
from jax import lax
from jax.experimental import pallas as pl
from jax.experimental.pallas import tpu as pltpu

D_MODEL = 1024
HEAD_DIM = 64
N_HEADS_A = 8
N_HEADS_B = 4
WIDTH = 512
D_FF = 2816
DEPTH = 2
PATTERNS = ((128, 1), (512, 4), (2048, 16))
DILATIONS = tuple(d for _, d in PATTERNS)
BAND = 64
N_BUCKETS = 32
MAX_DISTANCE = 1024
EPS = 1e-6
NEG = -1e30
LOG2E = 1.4426950408889634
Q_SCALE = LOG2E * HEAD_DIM ** -0.5

LANES = 128
BAND_TQ = 128
BAND_TK = BAND_TQ + 2 * BAND
BAND_STEP = 512
DIFF_TQ = 256
DIFF_TK = 512
DIFF_TILE_LO = -5
DIFF_TILE_HI = 4
PROJ_TILE = DIFF_TK
FFN_TILE = 256
FF_CHUNK = 256
VMEM_LIMIT = 56 * 1024 * 1024

F32 = jnp.float32
BF16 = jnp.bfloat16
NT_DIMS = (((1,), (1,)), ((), ()))


def _params(*sem):
    return pltpu.CompilerParams(dimension_semantics=sem, vmem_limit_bytes=VMEM_LIMIT)


def _t5_bucket_np(rel):
    nb = N_BUCKETS // 2
    max_exact = nb // 2
    ret = (rel > 0).astype(np.int32) * nb
    n = np.abs(rel)
    nf = np.maximum(n, 1).astype(np.float32)
    large = max_exact + (np.log(nf / np.float32(max_exact))
                         / np.float32(math.log(MAX_DISTANCE / max_exact))
                         * np.float32(nb - max_exact)).astype(np.int32)
    large = np.minimum(large, nb - 1)
    return ret + np.where(n < max_exact, n, large)


def _toeplitz_offsets(rows, cols):
    t = np.arange(rows + cols)
    return np.where(t < cols, t, t - (rows + cols))


def _toeplitz(vecs, rows, cols):
    p = rows + cols
    lead = vecs.shape[:-1]
    flat = jnp.tile(vecs, (1,) * len(lead) + (rows,))
    return flat[..., :rows * (p - 1)].reshape(lead + (rows, p - 1))[..., :cols]


def _band_bias_tiles(table_a):
    tiles = []
    for d in DILATIONS:
        delta = _toeplitz_offsets(BAND_TQ, BAND_TK) - BAND
        idx = _t5_bucket_np(delta * d)
        vec = jnp.where(jnp.asarray(np.abs(delta) <= BAND)[None, :],
                        table_a.T[:, idx] * LOG2E, NEG)
        tiles.append(_toeplitz(vec, BAND_TQ, BAND_TK))
    return jnp.stack(tiles)


def _diff_bias_tiles(table_b):
    vecs = []
    for t in range(DIFF_TILE_LO, DIFF_TILE_HI + 1):
        delta = t * DIFF_TQ - _toeplitz_offsets(DIFF_TK, DIFF_TQ)
        vecs.append(table_b.T[:, _t5_bucket_np(delta)] * LOG2E)
    return _toeplitz(jnp.stack(vecs, axis=1), DIFF_TK, DIFF_TQ)


def _mod_kernel(c_ref, w_ref, b_ref, o_ref):
    c = c_ref[...]
    s = c * jax.nn.sigmoid(c)
    o_ref[0] = jnp.dot(s, w_ref[0], precision=lax.Precision.HIGHEST,
                       preferred_element_type=F32) + b_ref[0]


def _modulation(c_all, ada_w, ada_b):
    rows = c_all.shape[0]
    n_col = 6
    return pl.pallas_call(
        _mod_kernel,
        grid=(DEPTH, n_col),
        in_specs=[
            pl.BlockSpec((rows, D_MODEL), lambda l, j: (0, 0)),
            pl.BlockSpec((1, D_MODEL, D_MODEL), lambda l, j: (l, 0, j)),
            pl.BlockSpec((1, 1, D_MODEL), lambda l, j: (l, 0, j)),
        ],
        out_specs=pl.BlockSpec((1, rows, D_MODEL), lambda l, j: (l, 0, j)),
        out_shape=jax.ShapeDtypeStruct((DEPTH, rows, n_col * D_MODEL), F32),
        compiler_params=_params("arbitrary", "arbitrary"),
        name="adaln_modulation",
    )(c_all, ada_w, ada_b.reshape(DEPTH, 1, n_col * D_MODEL))


def _rms(x, g):
    return x * lax.rsqrt(jnp.mean(x * x, axis=-1, keepdims=True) + EPS) * g


def _inproj_kernel(x_ref, mod_ref, g_ref, wn_ref, wt_ref, *refs):
    nat, sub4, sub16 = refs[0:3], refs[3:6], refs[6:9]
    kb_ref, qbt_ref, vbt_ref, stage_ref = refs[9:13]
    tm = x_ref.shape[1]
    x = x_ref[0]
    h = _rms(x, g_ref[...]) * (1.0 + mod_ref[0, 1:2, :]) + mod_ref[0, 0:1, :]
    h = h.astype(BF16)
    for i in range(3):
        p = jnp.dot(h, wn_ref[:, i * WIDTH:(i + 1) * WIDTH], preferred_element_type=F32)
        if i == 0:
            p = p * Q_SCALE
        nat[i][0] = p.astype(BF16)
        for grp in range(WIDTH // LANES):
            lanes = slice(grp * LANES, (grp + 1) * LANES)
            stage_ref[grp] = p[:, lanes]
            for d, subs in ((DILATIONS[1], sub4), (DILATIONS[2], sub16)):
                for r in range(d):
                    subs[i][0, r, :, lanes] = stage_ref[grp, pl.ds(r, tm // d, stride=d),
                                                        :].astype(BF16)
    kb_ref[0] = jnp.dot(h, wn_ref[:, 3 * WIDTH:4 * WIDTH],
                        preferred_element_type=F32).astype(BF16)
    qt = lax.dot_general(wt_ref[0:WIDTH, :], h, NT_DIMS, preferred_element_type=F32) * Q_SCALE
    for grp in range(WIDTH // LANES):
        for c in range(tm // DIFF_TQ):
            qbt_ref[0, grp, c] = qt[grp * LANES:(grp + 1) * LANES,
                                    c * DIFF_TQ:(c + 1) * DIFF_TQ].astype(BF16)
    vt = lax.dot_general(wt_ref[WIDTH:2 * WIDTH, :], h, NT_DIMS, preferred_element_type=F32)
    for grp in range(N_HEADS_B):
        for c in range(tm // DIFF_TK):
            vbt_ref[0, grp, c] = vt[grp * LANES:(grp + 1) * LANES,
                                    c * DIFF_TK:(c + 1) * DIFF_TK].astype(BF16)


def _inproj(x, mod, g, w_nat, w_t):
    b, l, _ = x.shape
    tm = PROJ_TILE
    groups = WIDTH // LANES
    nat = jax.ShapeDtypeStruct((b, l, WIDTH), BF16)
    nat_spec = pl.BlockSpec((1, tm, WIDTH), lambda i, j: (i, j, 0))
    sub = lambda d: jax.ShapeDtypeStruct((b, d, l // d, WIDTH), BF16)
    sub_spec = lambda d: pl.BlockSpec((1, d, tm // d, WIDTH), lambda i, j: (i, 0, j, 0))
    d4, d16 = DILATIONS[1], DILATIONS[2]
    return pl.pallas_call(
        _inproj_kernel,
        grid=(b, l // tm),
        in_specs=[
            pl.BlockSpec((1, tm, D_MODEL), lambda i, j: (i, j, 0)),
            pl.BlockSpec((1, 6, D_MODEL), lambda i, j: (i, 0, 0)),
            pl.BlockSpec((1, D_MODEL), lambda i, j: (0, 0)),
            pl.BlockSpec((D_MODEL, 4 * WIDTH), lambda i, j: (0, 0)),
            pl.BlockSpec((2 * WIDTH, D_MODEL), lambda i, j: (0, 0)),
        ],
        out_specs=[nat_spec] * 3 + [sub_spec(d4)] * 3 + [sub_spec(d16)] * 3 + [
            nat_spec,
            pl.BlockSpec((1, groups, tm // DIFF_TQ, LANES, DIFF_TQ), lambda i, j: (i, 0, j, 0, 0)),
            pl.BlockSpec((1, N_HEADS_B, tm // DIFF_TK, LANES, DIFF_TK), lambda i, j: (i, 0, j, 0, 0)),
        ],
        out_shape=[nat] * 3 + [sub(d4)] * 3 + [sub(d16)] * 3 + [
            nat,
            jax.ShapeDtypeStruct((b, groups, l // DIFF_TQ, LANES, DIFF_TQ), BF16),
            jax.ShapeDtypeStruct((b, N_HEADS_B, l // DIFF_TK, LANES, DIFF_TK), BF16),
        ],
        scratch_shapes=[pltpu.VMEM((groups, tm, LANES), F32)],
        compiler_params=_params("arbitrary", "arbitrary"),
        name="inproj",
    )(x, mod, g, w_nat, w_t)


def _lane_half_mask(half):
    lane = lax.broadcasted_iota(jnp.int32, (1, LANES), 1)
    return (lane // HEAD_DIM == half).astype(F32)


def _band_kernel(q_ref, kp_ref, kc_ref, kn_ref, vp_ref, vc_ref, vn_ref, bias_ref,
                 o_ref, lse_ref):
    j = pl.program_id(1)
    last = pl.num_programs(1) - 1
    n_sub = q_ref.shape[1] // BAND_TQ
    col = lax.broadcasted_iota(jnp.int32, (1, BAND_TK), 1)
    lo_edge = jnp.where(col < BAND, jnp.where(j == 0, NEG, 0.0), 0.0)
    hi_edge = jnp.where(col >= BAND + BAND_TQ, jnp.where(j == last, NEG, 0.0), 0.0)
    lane = lax.broadcasted_iota(jnp.int32, (1, LANES), 1)
    for pair in range(N_HEADS_A // 2):
        sl = slice(pair * LANES, (pair + 1) * LANES)
        k = jnp.concatenate([kp_ref[0, :, sl], kc_ref[0, :, sl], kn_ref[0, :, sl]], axis=0)
        v = jnp.concatenate([vp_ref[0, :, sl], vc_ref[0, :, sl], vn_ref[0, :, sl]], axis=0)
        for sb in range(n_sub):
            rows = slice(sb * BAND_TQ, (sb + 1) * BAND_TQ)
            q = q_ref[0, rows, sl].astype(F32)
            k_sb = k[sb * BAND_TQ:sb * BAND_TQ + BAND_TK]
            v_sb = v[sb * BAND_TQ:sb * BAND_TQ + BAND_TK]
            o_pair = None
            lse_pair = None
            for half in range(2):
                qm = (q * _lane_half_mask(half)).astype(BF16)
                s = lax.dot_general(qm, k_sb, NT_DIMS, preferred_element_type=F32)
                s = s + bias_ref[2 * pair + half]
                if sb == 0:
                    s = s + lo_edge
                if sb == n_sub - 1:
                    s = s + hi_edge
                m = jnp.max(s, axis=-1, keepdims=True)
                p = jnp.exp2(s - m)
                den = jnp.sum(p, axis=-1, keepdims=True)
                acc = jnp.dot(p.astype(BF16), v_sb, preferred_element_type=F32)
                o_h = acc / den
                lse_h = jnp.broadcast_to(m + jnp.log(den) * LOG2E, (BAND_TQ, LANES))
                if half == 0:
                    o_pair, lse_pair = o_h, lse_h
                else:
                    o_pair = jnp.where(lane < HEAD_DIM, o_pair, o_h)
                    lse_pair = jnp.where(lane < HEAD_DIM, lse_pair, lse_h)
            o_ref[0, rows, sl] = o_pair
            lse_ref[0, rows, sl] = lse_pair


def _band_attention(q, k, v, bias):
    s, n, _ = q.shape
    step = min(n, BAND_STEP)
    nq = n // step
    nk = n // BAND
    per = step // BAND
    cur = pl.BlockSpec((1, step, WIDTH), lambda i, j: (i, j, 0))
    prev = pl.BlockSpec((1, BAND, WIDTH), lambda i, j: (i, jnp.maximum(j * per - 1, 0), 0))
    nxt = pl.BlockSpec((1, BAND, WIDTH), lambda i, j: (i, jnp.minimum((j + 1) * per, nk - 1), 0))
    out = jax.ShapeDtypeStruct((s, n, WIDTH), F32)
    return pl.pallas_call(
        _band_kernel,
        grid=(s, nq),
        in_specs=[cur, prev, cur, nxt, prev, cur, nxt,
                  pl.BlockSpec((N_HEADS_A, BAND_TQ, BAND_TK), lambda i, j: (0, 0, 0))],
        out_specs=[cur, cur],
        out_shape=[out, out],
        compiler_params=_params("arbitrary", "arbitrary"),
        name="band_attention",
    )(q, k, k, k, v, v, v, bias)


def _diff_kernel(lam_init, q0_ref, q1_ref, k0_ref, k1_ref, vt_ref, bias_ref,
                 lq1_ref, lk1_ref, lq2_ref, lk2_ref, g_ref, o_ref, a0_ref, a1_ref, s_ref):
    head = pl.program_id(1)
    qi = pl.program_id(2)
    n_kv = k0_ref.shape[1] // DIFF_TK
    row = lax.broadcasted_iota(jnp.int32, (LANES, 1), 0)
    mask = (row // HEAD_DIM == head % 2).astype(F32)
    q0t = (q0_ref[0, 0, 0].astype(F32) * mask).astype(BF16)
    q1t = (q1_ref[0, 0, 0].astype(F32) * mask).astype(BF16)
    a0_ref[...] = jnp.zeros(a0_ref.shape, F32)
    a1_ref[...] = jnp.zeros(a1_ref.shape, F32)
    maps = ((q0t, k0_ref, a0_ref), (q1t, k1_ref, a1_ref))

    def scores(kk, slot):
        start = pl.multiple_of(kk * DIFF_TK, DIFF_TK)
        off = kk * (DIFF_TK // DIFF_TQ) - qi
        bias = bias_ref[0, jnp.clip(off, DIFF_TILE_LO, DIFF_TILE_HI) - DIFF_TILE_LO]
        cmax = []
        for m, (qt, k_ref, _) in enumerate(maps):
            s = jnp.dot(k_ref[0, pl.ds(start, DIFF_TK), :], qt, preferred_element_type=F32) + bias
            s_ref[slot, m] = s
            cmax.append(jnp.max(s, axis=0, keepdims=True))
        return tuple(cmax)

    def consume(kk, slot, cmax, stats):
        vt = vt_ref[0, 0, kk]
        out = []
        for m, (_, _, a_ref) in enumerate(maps):
            m_old, l_old = stats[m]
            m_new = jnp.maximum(m_old, cmax[m])
            alpha = jnp.exp2(m_old - m_new)
            p = jnp.exp2(s_ref[slot, m] - m_new)
            l_new = alpha * l_old + jnp.sum(p, axis=0, keepdims=True)
            a_ref[...] = alpha * a_ref[...] + jnp.dot(vt, p.astype(BF16),
                                                      preferred_element_type=F32)
            out.append((m_new, l_new))
        return tuple(out)

    def pair(j, carry):
        cmax, stats = carry
        nxt = scores(2 * j + 1, 1)
        stats = consume(2 * j, 0, cmax, stats)
        cmax = scores(2 * j + 2, 0)
        stats = consume(2 * j + 1, 1, nxt, stats)
        return cmax, stats

    stat = (jnp.full((1, DIFF_TQ), NEG, F32), jnp.zeros((1, DIFF_TQ), F32))
    cmax, stats = lax.fori_loop(0, n_kv // 2 - 1, pair, (scores(0, 0), (stat, stat)))
    nxt = scores(n_kv - 1, 1)
    stats = consume(n_kv - 2, 0, cmax, stats)
    (_, l0), (_, l1) = consume(n_kv - 1, 1, nxt, stats)

    lam = (jnp.exp(jnp.sum(lq1_ref[...] * lk1_ref[...], axis=-1, keepdims=True))
           - jnp.exp(jnp.sum(lq2_ref[...] * lk2_ref[...], axis=-1, keepdims=True))
           + lam_init)
    o = (a0_ref[...] / l0 - lam * (a1_ref[...] / l1)).T
    o_ref[0] = (_rms(o, g_ref[...]) * (1.0 - lam_init)).astype(BF16)


def _diff_attention(qbt, kb, vbt, bias, lq1, lk1, lq2, lk2, g, lam_init):
    b, l, _ = kb.shape
    nq = l // DIFF_TQ
    nk = l // DIFF_TK
    n_tiles = bias.shape[1]
    pairs = N_HEADS_B // 2
    qspec = lambda m: pl.BlockSpec((1, 1, 1, LANES, DIFF_TQ),
                                   lambda i, h, j: (i, m * pairs + h // 2, j, 0, 0))
    kspec = lambda m: pl.BlockSpec((1, l, LANES), lambda i, h, j: (i, 0, m * pairs + h // 2))
    small = pl.BlockSpec((1, HEAD_DIM), lambda i, h, j: (0, 0))
    acc = pltpu.VMEM((LANES, DIFF_TQ), F32)
    return pl.pallas_call(
        functools.partial(_diff_kernel, lam_init),
        grid=(b, N_HEADS_B, nq),
        in_specs=[
            qspec(0), qspec(1), kspec(0), kspec(1),
            pl.BlockSpec((1, 1, nk, LANES, DIFF_TK), lambda i, h, j: (i, h, 0, 0, 0)),
            pl.BlockSpec((1, n_tiles, DIFF_TK, DIFF_TQ), lambda i, h, j: (h, 0, 0, 0)),
            small, small, small, small,
            pl.BlockSpec((1, 2 * HEAD_DIM), lambda i, h, j: (0, 0)),
        ],
        out_specs=pl.BlockSpec((1, DIFF_TQ, LANES), lambda i, h, j: (i, j, h)),
        out_shape=jax.ShapeDtypeStruct((b, l, WIDTH), BF16),
        scratch_shapes=[acc, acc, pltpu.VMEM((2, 2, DIFF_TK, DIFF_TQ), F32)],
        compiler_params=_params("arbitrary", "arbitrary", "arbitrary"),
        name="diff_attention",
    )(qbt, qbt, kb, kb, vbt, bias, lq1, lk1, lq2, lk2, g)


def _outproj_kernel(x_ref, o1_ref, l1_ref, o4_ref, l4_ref, o16_ref, l16_ref, ob_ref,
                    mod_ref, w_ref, y_ref, so4_ref, sl4_ref, so16_ref, sl16_ref, oa_ref):
    tm = x_ref.shape[1]
    for grp in range(WIDTH // LANES):
        lanes = slice(grp * LANES, (grp + 1) * LANES)
        for d, src, dst in ((DILATIONS[1], o4_ref, so4_ref), (DILATIONS[1], l4_ref, sl4_ref),
                            (DILATIONS[2], o16_ref, so16_ref), (DILATIONS[2], l16_ref, sl16_ref)):
            for r in range(d):
                dst[grp, pl.ds(r, tm // d, stride=d), :] = src[0, r, :, lanes]
        l1, l2, l3 = l1_ref[0, :, lanes], sl4_ref[grp], sl16_ref[grp]
        mx = jnp.maximum(jnp.maximum(l1, l2), l3)
        e1, e2, e3 = jnp.exp2(l1 - mx), jnp.exp2(l2 - mx), jnp.exp2(l3 - mx)
        oa = (e1 * o1_ref[0, :, lanes] + e2 * so4_ref[grp] + e3 * so16_ref[grp]) / (e1 + e2 + e3)
        oa_ref[:, lanes] = oa.astype(BF16)
    mix = (jnp.dot(oa_ref[...], w_ref[0:WIDTH, :], preferred_element_type=F32)
           + jnp.dot(ob_ref[0], w_ref[WIDTH:2 * WIDTH, :], preferred_element_type=F32))
    y_ref[0] = x_ref[0] + mod_ref[0, 2:3, :] * mix


def _outproj(x, band_out, ob, mod, w_out):
    b, l, _ = x.shape
    tm = PROJ_TILE
    xs = pl.BlockSpec((1, tm, D_MODEL), lambda i, j: (i, j, 0))
    hs = pl.BlockSpec((1, tm, WIDTH), lambda i, j: (i, j, 0))
    sub = lambda d: pl.BlockSpec((1, d, tm // d, WIDTH), lambda i, j: (i, 0, j, 0))
    d4, d16 = DILATIONS[1], DILATIONS[2]
    stage = pltpu.VMEM((WIDTH // LANES, tm, LANES), F32)
    (o1, l1), (o4, l4), (o16, l16) = band_out
    return pl.pallas_call(
        _outproj_kernel,
        grid=(b, l // tm),
        in_specs=[xs, hs, hs, sub(d4), sub(d4), sub(d16), sub(d16), hs,
                  pl.BlockSpec((1, 6, D_MODEL), lambda i, j: (i, 0, 0)),
                  pl.BlockSpec((2 * WIDTH, D_MODEL), lambda i, j: (0, 0))],
        out_specs=xs,
        out_shape=jax.ShapeDtypeStruct(x.shape, F32),
        scratch_shapes=[stage] * 4 + [pltpu.VMEM((tm, WIDTH), BF16)],
        compiler_params=_params("arbitrary", "arbitrary"),
        name="outproj",
    )(x, o1, l1, o4, l4, o16, l16, ob, mod, w_out)


def _ffn_kernel(final, x_ref, mod_ref, g_ref, wg_ref, wu_ref, wd_ref, fg_ref, y_ref, act_ref):
    x = x_ref[0]
    h = _rms(x, g_ref[...]) * (1.0 + mod_ref[0, 4:5, :]) + mod_ref[0, 3:4, :]
    h = h.astype(BF16)
    for c in range(D_FF // FF_CHUNK):
        sl = slice(c * FF_CHUNK, (c + 1) * FF_CHUNK)
        gate = jnp.dot(h, wg_ref[:, sl], preferred_element_type=F32)
        up = jnp.dot(h, wu_ref[:, sl], preferred_element_type=F32)
        act_ref[:, sl] = (gate * jax.nn.sigmoid(gate) * up).astype(BF16)
    ff = jnp.dot(act_ref[...], wd_ref[...], preferred_element_type=F32)
    y = x + mod_ref[0, 5:6, :] * ff
    if final:
        y = _rms(y, fg_ref[...])
    y_ref[0] = y


def _ffn(x, mod, g, w_gate, w_up, w_down, final_g, final):
    b, l, _ = x.shape
    tm = FFN_TILE
    xs = pl.BlockSpec((1, tm, D_MODEL), lambda i, j: (i, j, 0))
    vec = pl.BlockSpec((1, D_MODEL), lambda i, j: (0, 0))
    once = pl.Buffered(1)
    return pl.pallas_call(
        functools.partial(_ffn_kernel, final),
        grid=(b, l // tm),
        in_specs=[xs, pl.BlockSpec((1, 6, D_MODEL), lambda i, j: (i, 0, 0)), vec,
                  pl.BlockSpec((D_MODEL, D_FF), lambda i, j: (0, 0), pipeline_mode=once),
                  pl.BlockSpec((D_MODEL, D_FF), lambda i, j: (0, 0), pipeline_mode=once),
                  pl.BlockSpec((D_FF, D_MODEL), lambda i, j: (0, 0), pipeline_mode=once),
                  vec],
        out_specs=xs,
        out_shape=jax.ShapeDtypeStruct(x.shape, F32),
        scratch_shapes=[pltpu.VMEM((tm, D_FF), BF16)],
        compiler_params=_params("arbitrary", "arbitrary"),
        name="ffn",
    )(x, mod, g, w_gate, w_up, w_down, final_g)


def _trunk(x, mods, band_bias, diff_bias, p):
    b, l, _ = x.shape
    for layer in range(DEPTH):
        mod = mods[layer]
        outs = _inproj(x, mod, p["norm_mix_g"][layer:layer + 1], p["w_nat"][layer], p["w_t"][layer])
        kb, qbt, vbt = outs[9:12]
        band_out = []
        for i, d in enumerate(DILATIONS):
            q, k, v = (t.reshape(b * d, l // d, WIDTH) for t in outs[3 * i:3 * i + 3])
            o, lse = _band_attention(q, k, v, band_bias[i])
            shape = (b, l, WIDTH) if d == 1 else (b, d, l // d, WIDTH)
            band_out.append((o.reshape(shape), lse.reshape(shape)))
        lam_init = 0.8 - 0.6 * math.exp(-0.3 * layer)
        ob = _diff_attention(qbt, kb, vbt, diff_bias,
                             p["lambda_q1"][layer:layer + 1], p["lambda_k1"][layer:layer + 1],
                             p["lambda_q2"][layer:layer + 1], p["lambda_k2"][layer:layer + 1],
                             p["subln_g"][layer:layer + 1], lam_init)
        x = _outproj(x, band_out, ob, mod, p["w_out"][layer])
        x = _ffn(x, mod, p["norm_ffn_g"][layer:layer + 1], p["w_gate"][layer], p["w_up"][layer],
                 p["w_down"][layer], p["final_g"], final=(layer == DEPTH - 1))
    return x


def kernel(x_prompt, x_sample, c_prompt, c_sample, rel_bias, ada_w, ada_b, norm_mix_g,
           norm_ffn_g, w_in, w_out, lambda_q1, lambda_k1, lambda_q2, lambda_k2, subln_g,
           w_gate, w_up, w_down, final_g):
    nb_p, nb_s = c_prompt.shape[0], c_sample.shape[0]
    rows = -(-(nb_p + nb_s) // 8) * 8
    c_all = jnp.zeros((rows, D_MODEL), F32).at[:nb_p].set(c_prompt).at[nb_p:nb_p + nb_s].set(c_sample)
    mods = _modulation(c_all, ada_w, ada_b).reshape(DEPTH, rows, 6, D_MODEL)
    band_bias = _band_bias_tiles(rel_bias[:, :N_HEADS_A])
    diff_bias = _diff_bias_tiles(rel_bias[:, N_HEADS_A:])
    w_bf = w_in.astype(BF16)
    w_nat = jnp.concatenate([w_bf[:, :, :3 * WIDTH], w_bf[:, :, 4 * WIDTH:5 * WIDTH]], axis=2)
    w_t = jnp.concatenate([w_bf[:, :, 3 * WIDTH:4 * WIDTH], w_bf[:, :, 5 * WIDTH:]],
                          axis=2).transpose(0, 2, 1)
    p = dict(norm_mix_g=norm_mix_g, norm_ffn_g=norm_ffn_g, w_nat=w_nat, w_t=w_t,
             w_out=w_out.astype(BF16), lambda_q1=lambda_q1, lambda_k1=lambda_k1,
             lambda_q2=lambda_q2, lambda_k2=lambda_k2, subln_g=subln_g,
             w_gate=w_gate.astype(BF16), w_up=w_up.astype(BF16), w_down=w_down.astype(BF16),
             final_g=final_g.reshape(1, D_MODEL))
    y_prompt = _trunk(x_prompt, mods[:, :nb_p], band_bias, diff_bias, p)
    y_sample = _trunk(x_sample, mods[:, nb_p:nb_p + nb_s], band_bias, diff_bias, p)
    return (y_prompt, y_sample)
```

```python
import functools
import math

import jax
import jax.numpy as jnp
import numpy as np
from jax import lax
from jax.experimental import pallas as pl
from jax.experimental.pallas import tpu as pltpu

D_MODEL = 1024
HEAD_DIM = 64
N_HEADS_A = 8
N_HEADS_B = 4
WIDTH = 512
D_FF = 2816
DEPTH = 2
PATTERNS = ((128, 1), (512, 4), (2048, 16))
DILATIONS = tuple(d for _, d in PATTERNS)
BAND = 64
N_BUCKETS = 32
MAX_DISTANCE = 1024
EPS = 1e-6
NEG = -1e30
LOG2E = 1.4426950408889634
Q_SCALE = LOG2E * HEAD_DIM ** -0.5

LANES = 128
BAND_TQ = 128
BAND_TK = BAND_TQ + 2 * BAND
BAND_STEP = 512
DIFF_TQ = 512
DIFF_TK = 512
DIFF_TILE_LO = -3
DIFF_TILE_HI = 3
V_ROWS = LANES + 16
PROJ_TILE = DIFF_TK
FFN_TILE = 256
FF_CHUNK = 256
VMEM_LIMIT = 56 * 1024 * 1024

F32 = jnp.float32
BF16 = jnp.bfloat16
NT_DIMS = (((1,), (1,)), ((), ()))


def _params(*sem):
    return pltpu.CompilerParams(dimension_semantics=sem, vmem_limit_bytes=VMEM_LIMIT)


def _t5_bucket_np(rel):
    nb = N_BUCKETS // 2
    max_exact = nb // 2
    ret = (rel > 0).astype(np.int32) * nb
    n = np.abs(rel)
    nf = np.maximum(n, 1).astype(np.float32)
    large = max_exact + (np.log(nf / np.float32(max_exact))
                         / np.float32(math.log(MAX_DISTANCE / max_exact))
                         * np.float32(nb - max_exact)).astype(np.int32)
    large = np.minimum(large, nb - 1)
    return ret + np.where(n < max_exact, n, large)


def _toeplitz_offsets(rows, cols):
    t = np.arange(rows + cols)
    return np.where(t < cols, t, t - (rows + cols))


def _toeplitz_kernel(v_ref, o_ref):
    rows, cols = o_ref.shape[1:]
    x = jnp.broadcast_to(v_ref[0], (rows, rows + cols))
    o_ref[0] = pltpu.roll(x, 0, 1, stride=1, stride_axis=0)[:, :cols]


def _toeplitz(vecs, rows, cols):
    p = rows + cols
    lead = vecs.shape[:-1]
    n = math.prod(lead)
    out = pl.pallas_call(
        _toeplitz_kernel,
        grid=(n,),
        in_specs=[pl.BlockSpec((1, 1, p), lambda i: (i, 0, 0))],
        out_specs=pl.BlockSpec((1, rows, cols), lambda i: (i, 0, 0)),
        out_shape=jax.ShapeDtypeStruct((n, rows, cols), F32),
        compiler_params=_params("arbitrary"),
        name="toeplitz_bias",
    )(vecs.reshape(n, 1, p))
    return out.reshape(lead + (rows, cols))


def _band_bias_tiles(table_a):
    vecs = []
    for d in DILATIONS:
        delta = _toeplitz_offsets(BAND_TQ, BAND_TK) - BAND
        idx = _t5_bucket_np(delta * d)
        vecs.append(jnp.where(jnp.asarray(np.abs(delta) <= BAND)[None, :],
                              table_a.T[:, idx] * LOG2E, NEG))
    return _toeplitz(jnp.stack(vecs), BAND_TQ, BAND_TK)


def _diff_bias_tiles(table_b):
    vecs = []
    for t in range(DIFF_TILE_LO, DIFF_TILE_HI + 1):
        delta = t * DIFF_TQ - _toeplitz_offsets(DIFF_TK, DIFF_TQ)
        vecs.append(table_b.T[:, _t5_bucket_np(delta)] * LOG2E)
    return _toeplitz(jnp.stack(vecs, axis=1), DIFF_TK, DIFF_TQ)


def _mod_kernel(c_ref, w_ref, b_ref, o_ref):
    c = c_ref[...]
    s = c * jax.nn.sigmoid(c)
    o_ref[0] = jnp.dot(s, w_ref[0], precision=lax.Precision.HIGHEST,
                       preferred_element_type=F32) + b_ref[0]


def _modulation(c_all, ada_w, ada_b):
    rows = c_all.shape[0]
    n_col = 6
    return pl.pallas_call(
        _mod_kernel,
        grid=(DEPTH, n_col),
        in_specs=[
            pl.BlockSpec((rows, D_MODEL), lambda l, j: (0, 0)),
            pl.BlockSpec((1, D_MODEL, D_MODEL), lambda l, j: (l, 0, j)),
            pl.BlockSpec((1, 1, D_MODEL), lambda l, j: (l, 0, j)),
        ],
        out_specs=pl.BlockSpec((1, rows, D_MODEL), lambda l, j: (l, 0, j)),
        out_shape=jax.ShapeDtypeStruct((DEPTH, rows, n_col * D_MODEL), F32),
        compiler_params=_params("arbitrary", "arbitrary"),
        name="adaln_modulation",
    )(c_all, ada_w, ada_b.reshape(DEPTH, 1, n_col * D_MODEL))


def _rms(x, g):
    return x * lax.rsqrt(jnp.mean(x * x, axis=-1, keepdims=True) + EPS) * g


def _inproj_kernel(x_ref, mod_ref, g_ref, wn_ref, wt_ref, *refs):
    nat, sub4, sub16 = refs[0:3], refs[3:6], refs[6:9]
    kb_ref, qbt_ref, vbt_ref, stage_ref = refs[9:13]
    tm = x_ref.shape[1]
    x = x_ref[0]
    h = _rms(x, g_ref[...]) * (1.0 + mod_ref[0, 1:2, :]) + mod_ref[0, 0:1, :]
    h = h.astype(BF16)
    for i in range(3):
        p = jnp.dot(h, wn_ref[:, i * WIDTH:(i + 1) * WIDTH], preferred_element_type=F32)
        if i == 0:
            p = p * Q_SCALE
        nat[i][0] = p.astype(BF16)
        for grp in range(WIDTH // LANES):
            lanes = slice(grp * LANES, (grp + 1) * LANES)
            stage_ref[grp] = p[:, lanes]
            for d, subs in ((DILATIONS[1], sub4), (DILATIONS[2], sub16)):
                for r in range(d):
                    subs[i][0, r, :, lanes] = stage_ref[grp, pl.ds(r, tm // d, stride=d),
                                                        :].astype(BF16)
    kb_ref[0] = jnp.dot(h, wn_ref[:, 3 * WIDTH:4 * WIDTH],
                        preferred_element_type=F32).astype(BF16)
    qt = lax.dot_general(wt_ref[0:WIDTH, :], h, NT_DIMS, preferred_element_type=F32) * Q_SCALE
    for grp in range(WIDTH // LANES):
        for c in range(tm // DIFF_TQ):
            qbt_ref[0, grp, c] = qt[grp * LANES:(grp + 1) * LANES,
                                    c * DIFF_TQ:(c + 1) * DIFF_TQ].astype(BF16)
    vt = lax.dot_general(wt_ref[WIDTH:2 * WIDTH, :], h, NT_DIMS, preferred_element_type=F32)
    for grp in range(N_HEADS_B):
        for c in range(tm // DIFF_TK):
            vbt_ref[0, grp, c, 0:LANES, :] = vt[grp * LANES:(grp + 1) * LANES,
                                                c * DIFF_TK:(c + 1) * DIFF_TK].astype(BF16)
            vbt_ref[0, grp, c, LANES:V_ROWS, :] = jnp.ones((V_ROWS - LANES, DIFF_TK), BF16)


def _inproj(x, mod, g, w_nat, w_t):
    b, l, _ = x.shape
    tm = PROJ_TILE
    groups = WIDTH // LANES
    nat = jax.ShapeDtypeStruct((b, l, WIDTH), BF16)
    nat_spec = pl.BlockSpec((1, tm, WIDTH), lambda i, j: (i, j, 0))
    sub = lambda d: jax.ShapeDtypeStruct((b, d, l // d, WIDTH), BF16)
    sub_spec = lambda d: pl.BlockSpec((1, d, tm // d, WIDTH), lambda i, j: (i, 0, j, 0))
    d4, d16 = DILATIONS[1], DILATIONS[2]
    return pl.pallas_call(
        _inproj_kernel,
        grid=(b, l // tm),
        in_specs=[
            pl.BlockSpec((1, tm, D_MODEL), lambda i, j: (i, j, 0)),
            pl.BlockSpec((1, 6, D_MODEL), lambda i, j: (i, 0, 0)),
            pl.BlockSpec((1, D_MODEL), lambda i, j: (0, 0)),
            pl.BlockSpec((D_MODEL, 4 * WIDTH), lambda i, j: (0, 0)),
            pl.BlockSpec((2 * WIDTH, D_MODEL), lambda i, j: (0, 0)),
        ],
        out_specs=[nat_spec] * 3 + [sub_spec(d4)] * 3 + [sub_spec(d16)] * 3 + [
            nat_spec,
            pl.BlockSpec((1, groups, tm // DIFF_TQ, LANES, DIFF_TQ), lambda i, j: (i, 0, j, 0, 0)),
            pl.BlockSpec((1, N_HEADS_B, tm // DIFF_TK, V_ROWS, DIFF_TK), lambda i, j: (i, 0, j, 0, 0)),
        ],
        out_shape=[nat] * 3 + [sub(d4)] * 3 + [sub(d16)] * 3 + [
            nat,
            jax.ShapeDtypeStruct((b, groups, l // DIFF_TQ, LANES, DIFF_TQ), BF16),
            jax.ShapeDtypeStruct((b, N_HEADS_B, l // DIFF_TK, V_ROWS, DIFF_TK), BF16),
        ],
        scratch_shapes=[pltpu.VMEM((groups, tm, LANES), F32)],
        compiler_params=_params("arbitrary", "arbitrary"),
        name="inproj",
    )(x, mod, g, w_nat, w_t)


def _lane_half_mask(half):
    lane = lax.broadcasted_iota(jnp.int32, (1, LANES), 1)
    return (lane // HEAD_DIM == half).astype(F32)


def _band_kernel(q_ref, kp_ref, kc_ref, kn_ref, vp_ref, vc_ref, vn_ref, bias_ref,
                 o_ref, lse_ref):
    j = pl.program_id(1)
    last = pl.num_programs(1) - 1
    n_sub = q_ref.shape[1] // BAND_TQ
    col = lax.broadcasted_iota(jnp.int32, (1, BAND_TK), 1)
    lo_edge = jnp.where(col < BAND, jnp.where(j == 0, NEG, 0.0), 0.0)
    hi_edge = jnp.where(col >= BAND + BAND_TQ, jnp.where(j == last, NEG, 0.0), 0.0)
    lane = lax.broadcasted_iota(jnp.int32, (1, LANES), 1)
    for g, pair in [(g, pair) for g in range(q_ref.shape[0]) for pair in range(N_HEADS_A // 2)]:
        sl = slice(pair * LANES, (pair + 1) * LANES)
        k = jnp.concatenate([kp_ref[g, :, sl], kc_ref[g, :, sl], kn_ref[g, :, sl]], axis=0)
        v = jnp.concatenate([vp_ref[g, :, sl], vc_ref[g, :, sl], vn_ref[g, :, sl]], axis=0)
        for sb in range(n_sub):
            rows = slice(sb * BAND_TQ, (sb + 1) * BAND_TQ)
            q = q_ref[g, rows, sl].astype(F32)
            k_sb = k[sb * BAND_TQ:sb * BAND_TQ + BAND_TK]
            v_sb = v[sb * BAND_TQ:sb * BAND_TQ + BAND_TK]
            o_pair = None
            lse_pair = None
            for half in range(2):
                qm = (q * _lane_half_mask(half)).astype(BF16)
                s = lax.dot_general(qm, k_sb, NT_DIMS, preferred_element_type=F32)
                s = s + bias_ref[2 * pair + half]
                if sb == 0:
                    s = s + lo_edge
                if sb == n_sub - 1:
                    s = s + hi_edge
                m = jnp.max(s, axis=-1, keepdims=True)
                p = jnp.exp2(s - m)
                den = jnp.sum(p, axis=-1, keepdims=True)
                acc = jnp.dot(p.astype(BF16), v_sb, preferred_element_type=F32)
                o_h = acc / den
                lse_h = jnp.broadcast_to(m + jnp.log(den) * LOG2E, (BAND_TQ, LANES))
                if half == 0:
                    o_pair, lse_pair = o_h, lse_h
                else:
                    o_pair = jnp.where(lane < HEAD_DIM, o_pair, o_h)
                    lse_pair = jnp.where(lane < HEAD_DIM, lse_pair, lse_h)
            o_ref[g, rows, sl] = o_pair
            lse_ref[g, rows, sl] = lse_pair


def _band_attention(q, k, v, bias):
    s, n, _ = q.shape
    step = min(n, BAND_STEP)
    grp = BAND_STEP // step
    nq = n // step
    nk = n // BAND
    per = step // BAND
    cur = pl.BlockSpec((grp, step, WIDTH), lambda i, j: (i, j, 0))
    prev = pl.BlockSpec((grp, BAND, WIDTH), lambda i, j: (i, jnp.maximum(j * per - 1, 0), 0))
    nxt = pl.BlockSpec((grp, BAND, WIDTH), lambda i, j: (i, jnp.minimum((j + 1) * per, nk - 1), 0))
    out = jax.ShapeDtypeStruct((s, n, WIDTH), F32)
    return pl.pallas_call(
        _band_kernel,
        grid=(s // grp, nq),
        in_specs=[cur, prev, cur, nxt, prev, cur, nxt,
                  pl.BlockSpec((N_HEADS_A, BAND_TQ, BAND_TK), lambda i, j: (0, 0, 0))],
        out_specs=[cur, cur],
        out_shape=[out, out],
        compiler_params=_params("arbitrary", "arbitrary"),
        name="band_attention",
    )(q, k, k, k, v, v, v, bias)


def _diff_kernel(lam_init, q0_ref, q1_ref, k0_ref, k1_ref, vt_ref, bias_ref,
                 lq1_ref, lk1_ref, lq2_ref, lk2_ref, g_ref, o_ref, a0_ref, a1_ref, s_ref):
    head = pl.program_id(1)
    qi = pl.program_id(2)
    n_kv = k0_ref.shape[1] // DIFF_TK
    row = lax.broadcasted_iota(jnp.int32, (LANES, 1), 0)
    mask = (row // HEAD_DIM == head % 2).astype(F32)
    q0t = (q0_ref[0, 0, 0].astype(F32) * mask).astype(BF16)
    q1t = (q1_ref[0, 0, 0].astype(F32) * mask).astype(BF16)
    a0_ref[...] = jnp.zeros(a0_ref.shape, F32)
    a1_ref[...] = jnp.zeros(a1_ref.shape, F32)
    maps = ((q0t, k0_ref, a0_ref), (q1t, k1_ref, a1_ref))

    def scores(kk, slot):
        start = pl.multiple_of(kk * DIFF_TK, DIFF_TK)
        off = kk * (DIFF_TK // DIFF_TQ) - qi
        bias = bias_ref[0, jnp.clip(off, DIFF_TILE_LO, DIFF_TILE_HI) - DIFF_TILE_LO]
        cmax = []
        for m, (qt, k_ref, _) in enumerate(maps):
            s = jnp.dot(k_ref[0, pl.ds(start, DIFF_TK), :], qt, preferred_element_type=F32) + bias
            s_ref[slot, m] = s
            cmax.append(jnp.max(s, axis=0, keepdims=True))
        return tuple(cmax)

    def consume(kk, slot, cmax, stats):
        vt = vt_ref[0, 0, kk]
        out = []
        for m, (_, _, a_ref) in enumerate(maps):
            m_new = jnp.maximum(stats[m], cmax[m])
            alpha = jnp.exp2(stats[m] - m_new)
            p = jnp.exp2(s_ref[slot, m] - m_new)
            a_ref[...] = alpha * a_ref[...] + jnp.dot(vt, p.astype(BF16),
                                                      preferred_element_type=F32)
            out.append(m_new)
        return tuple(out)

    def pair(j, carry):
        cmax, stats = carry
        nxt = scores(2 * j + 1, 1)
        stats = consume(2 * j, 0, cmax, stats)
        cmax = scores(2 * j + 2, 0)
        stats = consume(2 * j + 1, 1, nxt, stats)
        return cmax, stats

    stat = jnp.full((1, DIFF_TQ), NEG, F32)
    cmax, stats = lax.fori_loop(0, n_kv // 2 - 1, pair, (scores(0, 0), (stat, stat)))
    nxt = scores(n_kv - 1, 1)
    stats = consume(n_kv - 2, 0, cmax, stats)
    consume(n_kv - 1, 1, nxt, stats)

    lam = (jnp.exp(jnp.sum(lq1_ref[...] * lk1_ref[...], axis=-1, keepdims=True))
           - jnp.exp(jnp.sum(lq2_ref[...] * lk2_ref[...], axis=-1, keepdims=True))
           + lam_init)
    l0 = a0_ref[LANES:LANES + 1, :]
    l1 = a1_ref[LANES:LANES + 1, :]
    o = (a0_ref[0:LANES, :] / l0 - lam * (a1_ref[0:LANES, :] / l1)).T
    o_ref[0] = (_rms(o, g_ref[...]) * (1.0 - lam_init)).astype(BF16)


def _diff_attention(qbt, kb, vbt, bias, lq1, lk1, lq2, lk2, g, lam_init):
    b, l, _ = kb.shape
    nq = l // DIFF_TQ
    nk = l // DIFF_TK
    n_tiles = bias.shape[1]
    pairs = N_HEADS_B // 2
    qspec = lambda m: pl.BlockSpec((1, 1, 1, LANES, DIFF_TQ),
                                   lambda i, h, j: (i, m * pairs + h // 2, j, 0, 0))
    kspec = lambda m: pl.BlockSpec((1, l, LANES), lambda i, h, j: (i, 0, m * pairs + h // 2))
    small = pl.BlockSpec((1, HEAD_DIM), lambda i, h, j: (0, 0))
    acc = pltpu.VMEM((V_ROWS, DIFF_TQ), F32)
    return pl.pallas_call(
        functools.partial(_diff_kernel, lam_init),
        grid=(b, N_HEADS_B, nq),
        in_specs=[
            qspec(0), qspec(1), kspec(0), kspec(1),
            pl.BlockSpec((1, 1, nk, V_ROWS, DIFF_TK), lambda i, h, j: (i, h, 0, 0, 0)),
            pl.BlockSpec((1, n_tiles, DIFF_TK, DIFF_TQ), lambda i, h, j: (h, 0, 0, 0)),
            small, small, small, small,
            pl.BlockSpec((1, 2 * HEAD_DIM), lambda i, h, j: (0, 0)),
        ],
        out_specs=pl.BlockSpec((1, DIFF_TQ, LANES), lambda i, h, j: (i, j, h)),
        out_shape=jax.ShapeDtypeStruct((b, l, WIDTH), BF16),
        scratch_shapes=[acc, acc, pltpu.VMEM((2, 2, DIFF_TK, DIFF_TQ), F32)],
        compiler_params=_params("arbitrary", "arbitrary", "arbitrary"),
        name="diff_attention",
    )(qbt, qbt, kb, kb, vbt, bias, lq1, lk1, lq2, lk2, g)


def _outproj_kernel(x_ref, o1_ref, l1_ref, o4_ref, l4_ref, o16_ref, l16_ref, ob_ref,
                    mod_ref, w_ref, y_ref, so4_ref, sl4_ref, so16_ref, sl16_ref, oa_ref):
    tm = x_ref.shape[1]
    for grp in range(WIDTH // LANES):
        lanes = slice(grp * LANES, (grp + 1) * LANES)
        for d, src, dst in ((DILATIONS[1], o4_ref, so4_ref), (DILATIONS[1], l4_ref, sl4_ref),
                            (DILATIONS[2], o16_ref, so16_ref), (DILATIONS[2], l16_ref, sl16_ref)):
            for r in range(d):
                dst[grp, pl.ds(r, tm // d, stride=d), :] = src[0, r, :, lanes]
        l1, l2, l3 = l1_ref[0, :, lanes], sl4_ref[grp], sl16_ref[grp]
        mx = jnp.maximum(jnp.maximum(l1, l2), l3)
        e1, e2, e3 = jnp.exp2(l1 - mx), jnp.exp2(l2 - mx), jnp.exp2(l3 - mx)
        oa = (e1 * o1_ref[0, :, lanes] + e2 * so4_ref[grp] + e3 * so16_ref[grp]) / (e1 + e2 + e3)
        oa_ref[:, lanes] = oa.astype(BF16)
    mix = (jnp.dot(oa_ref[...], w_ref[0:WIDTH, :], preferred_element_type=F32)
           + jnp.dot(ob_ref[0], w_ref[WIDTH:2 * WIDTH, :], preferred_element_type=F32))
    y_ref[0] = x_ref[0] + mod_ref[0, 2:3, :] * mix


def _outproj(x, band_out, ob, mod, w_out):
    b, l, _ = x.shape
    tm = PROJ_TILE
    xs = pl.BlockSpec((1, tm, D_MODEL), lambda i, j: (i, j, 0))
    hs = pl.BlockSpec((1, tm, WIDTH), lambda i, j: (i, j, 0))
    sub = lambda d: pl.BlockSpec((1, d, tm // d, WIDTH), lambda i, j: (i, 0, j, 0))
    d4, d16 = DILATIONS[1], DILATIONS[2]
    stage = pltpu.VMEM((WIDTH // LANES, tm, LANES), F32)
    (o1, l1), (o4, l4), (o16, l16) = band_out
    return pl.pallas_call(
        _outproj_kernel,
        grid=(b, l // tm),
        in_specs=[xs, hs, hs, sub(d4), sub(d4), sub(d16), sub(d16), hs,
                  pl.BlockSpec((1, 6, D_MODEL), lambda i, j: (i, 0, 0)),
                  pl.BlockSpec((2 * WIDTH, D_MODEL), lambda i, j: (0, 0))],
        out_specs=xs,
        out_shape=jax.ShapeDtypeStruct(x.shape, F32),
        scratch_shapes=[stage] * 4 + [pltpu.VMEM((tm, WIDTH), BF16)],
        compiler_params=_params("arbitrary", "arbitrary"),
        name="outproj",
    )(x, o1, l1, o4, l4, o16, l16, ob, mod, w_out)


def _ffn_kernel(final, x_ref, mod_ref, g_ref, wg_ref, wu_ref, wd_ref, fg_ref, y_ref, act_ref):
    x = x_ref[0]
    h = _rms(x, g_ref[...]) * (1.0 + mod_ref[0, 4:5, :]) + mod_ref[0, 3:4, :]
    h = h.astype(BF16)
    for c in range(D_FF // FF_CHUNK):
        sl = slice(c * FF_CHUNK, (c + 1) * FF_CHUNK)
        gate = jnp.dot(h, wg_ref[:, sl], preferred_element_type=F32)
        up = jnp.dot(h, wu_ref[:, sl], preferred_element_type=F32)
        act_ref[:, sl] = (gate * jax.nn.sigmoid(gate) * up).astype(BF16)
    ff = jnp.dot(act_ref[...], wd_ref[...], preferred_element_type=F32)
    y = x + mod_ref[0, 5:6, :] * ff
    if final:
        y = _rms(y, fg_ref[...])
    y_ref[0] = y


def _ffn(x, mod, g, w_gate, w_up, w_down, final_g, final):
    b, l, _ = x.shape
    tm = FFN_TILE
    xs = pl.BlockSpec((1, tm, D_MODEL), lambda i, j: (i, j, 0))
    vec = pl.BlockSpec((1, D_MODEL), lambda i, j: (0, 0))
    once = pl.Buffered(1)
    return pl.pallas_call(
        functools.partial(_ffn_kernel, final),
        grid=(b, l // tm),
        in_specs=[xs, pl.BlockSpec((1, 6, D_MODEL), lambda i, j: (i, 0, 0)), vec,
                  pl.BlockSpec((D_MODEL, D_FF), lambda i, j: (0, 0), pipeline_mode=once),
                  pl.BlockSpec((D_MODEL, D_FF), lambda i, j: (0, 0), pipeline_mode=once),
                  pl.BlockSpec((D_FF, D_MODEL), lambda i, j: (0, 0), pipeline_mode=once),
                  vec],
        out_specs=xs,
        out_shape=jax.ShapeDtypeStruct(x.shape, F32),
        scratch_shapes=[pltpu.VMEM((tm, D_FF), BF16)],
        compiler_params=_params("arbitrary", "arbitrary"),
        name="ffn",
    )(x, mod, g, w_gate, w_up, w_down, final_g)


def _trunk(x, mods, band_bias, diff_bias, p):
    b, l, _ = x.shape
    for layer in range(DEPTH):
        mod = mods[layer]
        outs = _inproj(x, mod, p["norm_mix_g"][layer:layer + 1], p["w_nat"][layer], p["w_t"][layer])
        kb, qbt, vbt = outs[9:12]
        band_out = []
        for i, d in enumerate(DILATIONS):
            q, k, v = (t.reshape(b * d, l // d, WIDTH) for t in outs[3 * i:3 * i + 3])
            o, lse = _band_attention(q, k, v, band_bias[i])
            shape = (b, l, WIDTH) if d == 1 else (b, d, l // d, WIDTH)
            band_out.append((o.reshape(shape), lse.reshape(shape)))
        lam_init = 0.8 - 0.6 * math.exp(-0.3 * layer)
        ob = _diff_attention(qbt, kb, vbt, diff_bias,
                             p["lambda_q1"][layer:layer + 1], p["lambda_k1"][layer:layer + 1],
                             p["lambda_q2"][layer:layer + 1], p["lambda_k2"][layer:layer + 1],
                             p["subln_g"][layer:layer + 1], lam_init)
        x = _outproj(x, band_out, ob, mod, p["w_out"][layer])
        x = _ffn(x, mod, p["norm_ffn_g"][layer:layer + 1], p["w_gate"][layer], p["w_up"][layer],
                 p["w_down"][layer], p["final_g"], final=(layer == DEPTH - 1))
    return x


def kernel(x_prompt, x_sample, c_prompt, c_sample, rel_bias, ada_w, ada_b, norm_mix_g,
           norm_ffn_g, w_in, w_out, lambda_q1, lambda_k1, lambda_q2, lambda_k2, subln_g,
           w_gate, w_up, w_down, final_g):
    nb_p, nb_s = c_prompt.shape[0], c_sample.shape[0]
    rows = -(-(nb_p + nb_s) // 8) * 8
    c_all = jnp.zeros((rows, D_MODEL), F32).at[:nb_p].set(c_prompt).at[nb_p:nb_p + nb_s].set(c_sample)
    mods = _modulation(c_all, ada_w, ada_b).reshape(DEPTH, rows, 6, D_MODEL)
    band_bias = _band_bias_tiles(rel_bias[:, :N_HEADS_A])
    diff_bias = _diff_bias_tiles(rel_bias[:, N_HEADS_A:])
    w_bf = w_in.astype(BF16)
    w_nat = jnp.concatenate([w_bf[:, :, :3 * WIDTH], w_bf[:, :, 4 * WIDTH:5 * WIDTH]], axis=2)
    w_t = jnp.concatenate([w_bf[:, :, 3 * WIDTH:4 * WIDTH], w_bf[:, :, 5 * WIDTH:]],
                          axis=2).transpose(0, 2, 1)
    p = dict(norm_mix_g=norm_mix_g, norm_ffn_g=norm_ffn_g, w_nat=w_nat, w_t=w_t,
             w_out=w_out.astype(BF16), lambda_q1=lambda_q1, lambda_k1=lambda_k1,
             lambda_q2=lambda_q2, lambda_k2=lambda_k2, subln_g=subln_g,
             w_gate=w_gate.astype(BF16), w_up=w_up.astype(BF16), w_down=w_down.astype(BF16),
             final_g=final_g.reshape(1, D_MODEL))
    y_prompt = _trunk(x_prompt, mods[:, :nb_p], band_bias, diff_bias, p)
    y_sample = _trunk(x_sample, mods[:, nb_p:nb_p + nb_s], band_bias, diff_bias, p)
    return (y_prompt, y_sample)
```

```python
import functools
import math

import jax
import jax.numpy as jnp
import numpy as np
from jax import lax
from jax.experimental import pallas as pl
from jax.experimental.pallas import tpu as pltpu

D_MODEL = 1024
HEAD_DIM = 64
N_HEADS_A = 8
N_HEADS_B = 4
WIDTH = 512
D_FF = 2816
DEPTH = 2
PATTERNS = ((128, 1), (512, 4), (2048, 16))
DILATIONS = tuple(d for _, d in PATTERNS)
BAND = 64
N_BUCKETS = 32
MAX_DISTANCE = 1024
EPS = 1e-6
NEG = -1e30
LOG2E = 1.4426950408889634
Q_SCALE = LOG2E * HEAD_DIM ** -0.5

LANES = 128
BAND_TQ = 128
BAND_TK = BAND_TQ + 2 * BAND
BAND_STEP = 512
LSE_LANES = LANES // N_HEADS_A
DIFF_TQ = 512
DIFF_TK = 512
DIFF_TILE_LO = -3
DIFF_TILE_HI = 3
V_ROWS = LANES + 16
PROJ_TILE = DIFF_TK
FFN_TILE = 512
FF_CHUNK = 256
VMEM_LIMIT = 56 * 1024 * 1024

F32 = jnp.float32
BF16 = jnp.bfloat16
NT_DIMS = (((1,), (1,)), ((), ()))


def _params(*sem):
    return pltpu.CompilerParams(dimension_semantics=sem, vmem_limit_bytes=VMEM_LIMIT)


def _t5_bucket_np(rel):
    nb = N_BUCKETS // 2
    max_exact = nb // 2
    ret = (rel > 0).astype(np.int32) * nb
    n = np.abs(rel)
    nf = np.maximum(n, 1).astype(np.float32)
    large = max_exact + (np.log(nf / np.float32(max_exact))
                         / np.float32(math.log(MAX_DISTANCE / max_exact))
                         * np.float32(nb - max_exact)).astype(np.int32)
    large = np.minimum(large, nb - 1)
    return ret + np.where(n < max_exact, n, large)


def _toeplitz_offsets(rows, cols):
    t = np.arange(rows + cols)
    return np.where(t < cols, t, t - (rows + cols))


def _toeplitz_kernel(v_ref, o_ref):
    rows, cols = o_ref.shape[1:]
    x = jnp.broadcast_to(v_ref[0], (rows, rows + cols))
    o_ref[0] = pltpu.roll(x, 0, 1, stride=1, stride_axis=0)[:, :cols]


def _toeplitz(vecs, rows, cols):
    p = rows + cols
    lead = vecs.shape[:-1]
    n = math.prod(lead)
    out = pl.pallas_call(
        _toeplitz_kernel,
        grid=(n,),
        in_specs=[pl.BlockSpec((1, 1, p), lambda i: (i, 0, 0))],
        out_specs=pl.BlockSpec((1, rows, cols), lambda i: (i, 0, 0)),
        out_shape=jax.ShapeDtypeStruct((n, rows, cols), F32),
        compiler_params=_params("arbitrary"),
        name="toeplitz_bias",
    )(vecs.reshape(n, 1, p))
    return out.reshape(lead + (rows, cols))


def _band_bias_tiles(table_a):
    vecs = []
    for d in DILATIONS:
        delta = _toeplitz_offsets(BAND_TQ, BAND_TK) - BAND
        idx = _t5_bucket_np(delta * d)
        vecs.append(jnp.where(jnp.asarray(np.abs(delta) <= BAND)[None, :],
                              table_a.T[:, idx] * LOG2E, NEG))
    return _toeplitz(jnp.stack(vecs), BAND_TQ, BAND_TK)


def _diff_bias_tiles(table_b):
    vecs = []
    for t in range(DIFF_TILE_LO, DIFF_TILE_HI + 1):
        delta = t * DIFF_TQ - _toeplitz_offsets(DIFF_TK, DIFF_TQ)
        vecs.append(table_b.T[:, _t5_bucket_np(delta)] * LOG2E)
    return _toeplitz(jnp.stack(vecs, axis=1), DIFF_TK, DIFF_TQ)


def _mod_kernel(c_ref, w_ref, b_ref, o_ref):
    c = c_ref[...]
    s = c * jax.nn.sigmoid(c)
    o_ref[0] = jnp.dot(s, w_ref[0], precision=lax.Precision.HIGHEST,
                       preferred_element_type=F32) + b_ref[0]


def _modulation(c_all, ada_w, ada_b):
    rows = c_all.shape[0]
    n_col = 6
    return pl.pallas_call(
        _mod_kernel,
        grid=(DEPTH, n_col),
        in_specs=[
            pl.BlockSpec((rows, D_MODEL), lambda l, j: (0, 0)),
            pl.BlockSpec((1, D_MODEL, D_MODEL), lambda l, j: (l, 0, j)),
            pl.BlockSpec((1, 1, D_MODEL), lambda l, j: (l, 0, j)),
        ],
        out_specs=pl.BlockSpec((1, rows, D_MODEL), lambda l, j: (l, 0, j)),
        out_shape=jax.ShapeDtypeStruct((DEPTH, rows, n_col * D_MODEL), F32),
        compiler_params=_params("arbitrary", "arbitrary"),
        name="adaln_modulation",
    )(c_all, ada_w, ada_b.reshape(DEPTH, 1, n_col * D_MODEL))


def _rms(x, g):
    return x * lax.rsqrt(jnp.mean(x * x, axis=-1, keepdims=True) + EPS) * g


def _inproj_kernel(x_ref, mod_ref, g_ref, wn_ref, wt_ref, *refs):
    nat, sub4, sub16 = refs[0:3], refs[3:6], refs[6:9]
    kb_ref, qbt_ref, vbt_ref, stage_ref = refs[9:13]
    tm = x_ref.shape[1]
    x = x_ref[0]
    h = _rms(x, g_ref[...]) * (1.0 + mod_ref[0, 1:2, :]) + mod_ref[0, 0:1, :]
    h = h.astype(BF16)
    for i in range(3):
        p = jnp.dot(h, wn_ref[:, i * WIDTH:(i + 1) * WIDTH], preferred_element_type=F32)
        if i == 0:
            p = p * Q_SCALE
        nat[i][0] = p.astype(BF16)
        for grp in range(WIDTH // LANES):
            lanes = slice(grp * LANES, (grp + 1) * LANES)
            stage_ref[grp] = p[:, lanes]
            for d, subs in ((DILATIONS[1], sub4), (DILATIONS[2], sub16)):
                for r in range(d):
                    subs[i][0, r, :, lanes] = stage_ref[grp, pl.ds(r, tm // d, stride=d),
                                                        :].astype(BF16)
    kb_ref[0] = jnp.dot(h, wn_ref[:, 3 * WIDTH:4 * WIDTH],
                        preferred_element_type=F32).astype(BF16)
    qt = lax.dot_general(wt_ref[0:WIDTH, :], h, NT_DIMS, preferred_element_type=F32) * Q_SCALE
    for grp in range(WIDTH // LANES):
        for c in range(tm // DIFF_TQ):
            qbt_ref[0, grp, c] = qt[grp * LANES:(grp + 1) * LANES,
                                    c * DIFF_TQ:(c + 1) * DIFF_TQ].astype(BF16)
    vt = lax.dot_general(wt_ref[WIDTH:2 * WIDTH, :], h, NT_DIMS, preferred_element_type=F32)
    for grp in range(N_HEADS_B):
        for c in range(tm // DIFF_TK):
            vbt_ref[0, grp, c, 0:LANES, :] = vt[grp * LANES:(grp + 1) * LANES,
                                                c * DIFF_TK:(c + 1) * DIFF_TK].astype(BF16)
            vbt_ref[0, grp, c, LANES:V_ROWS, :] = jnp.ones((V_ROWS - LANES, DIFF_TK), BF16)


def _inproj(x, mod, g, w_nat, w_t):
    b, l, _ = x.shape
    tm = PROJ_TILE
    groups = WIDTH // LANES
    nat = jax.ShapeDtypeStruct((b, l, WIDTH), BF16)
    nat_spec = pl.BlockSpec((1, tm, WIDTH), lambda i, j: (i, j, 0))
    sub = lambda d: jax.ShapeDtypeStruct((b, d, l // d, WIDTH), BF16)
    sub_spec = lambda d: pl.BlockSpec((1, d, tm // d, WIDTH), lambda i, j: (i, 0, j, 0))
    d4, d16 = DILATIONS[1], DILATIONS[2]
    return pl.pallas_call(
        _inproj_kernel,
        grid=(b, l // tm),
        in_specs=[
            pl.BlockSpec((1, tm, D_MODEL), lambda i, j: (i, j, 0)),
            pl.BlockSpec((1, 6, D_MODEL), lambda i, j: (i, 0, 0)),
            pl.BlockSpec((1, D_MODEL), lambda i, j: (0, 0)),
            pl.BlockSpec((D_MODEL, 4 * WIDTH), lambda i, j: (0, 0)),
            pl.BlockSpec((2 * WIDTH, D_MODEL), lambda i, j: (0, 0)),
        ],
        out_specs=[nat_spec] * 3 + [sub_spec(d4)] * 3 + [sub_spec(d16)] * 3 + [
            nat_spec,
            pl.BlockSpec((1, groups, tm // DIFF_TQ, LANES, DIFF_TQ), lambda i, j: (i, 0, j, 0, 0)),
            pl.BlockSpec((1, N_HEADS_B, tm // DIFF_TK, V_ROWS, DIFF_TK), lambda i, j: (i, 0, j, 0, 0)),
        ],
        out_shape=[nat] * 3 + [sub(d4)] * 3 + [sub(d16)] * 3 + [
            nat,
            jax.ShapeDtypeStruct((b, groups, l // DIFF_TQ, LANES, DIFF_TQ), BF16),
            jax.ShapeDtypeStruct((b, N_HEADS_B, l // DIFF_TK, V_ROWS, DIFF_TK), BF16),
        ],
        scratch_shapes=[pltpu.VMEM((groups, tm, LANES), F32)],
        compiler_params=_params("arbitrary", "arbitrary"),
        name="inproj",
    )(x, mod, g, w_nat, w_t)


def _lane_half_mask(half):
    lane = lax.broadcasted_iota(jnp.int32, (1, LANES), 1)
    return jnp.where(lane // HEAD_DIM == half, 1.0, 0.0).astype(BF16)


def _band_kernel(q_ref, kp_ref, kc_ref, kn_ref, vp_ref, vc_ref, vn_ref, bias_ref,
                 o_ref, lse_ref):
    j = pl.program_id(1)
    last = pl.num_programs(1) - 1
    n_sub = q_ref.shape[1] // BAND_TQ
    col = lax.broadcasted_iota(jnp.int32, (1, BAND_TK), 1)
    lo_edge = jnp.where(col < BAND, jnp.where(j == 0, NEG, 0.0), 0.0)
    hi_edge = jnp.where(col >= BAND + BAND_TQ, jnp.where(j == last, NEG, 0.0), 0.0)
    lane = lax.broadcasted_iota(jnp.int32, (1, LANES), 1)
    for g, pair in [(g, pair) for g in range(q_ref.shape[0]) for pair in range(N_HEADS_A // 2)]:
        sl = slice(pair * LANES, (pair + 1) * LANES)
        k = jnp.concatenate([kp_ref[g, :, sl], kc_ref[g, :, sl], kn_ref[g, :, sl]], axis=0)
        v = jnp.concatenate([vp_ref[g, :, sl], vc_ref[g, :, sl], vn_ref[g, :, sl]], axis=0)
        for sb in range(n_sub):
            rows = slice(sb * BAND_TQ, (sb + 1) * BAND_TQ)
            q = q_ref[g, rows, sl]
            k_sb = k[sb * BAND_TQ:sb * BAND_TQ + BAND_TK]
            v_sb = v[sb * BAND_TQ:sb * BAND_TQ + BAND_TK]
            o_pair = None
            lse_all = lse_ref[g, rows, :] if pair else jnp.zeros((BAND_TQ, LANES), F32)
            for half in range(2):
                qm = q * _lane_half_mask(half)
                s = lax.dot_general(qm, k_sb, NT_DIMS, preferred_element_type=F32)
                s = s + bias_ref[2 * pair + half]
                if sb == 0:
                    s = s + lo_edge
                if sb == n_sub - 1:
                    s = s + hi_edge
                m = jnp.max(s, axis=-1, keepdims=True)
                p = jnp.exp2(s - m)
                den = jnp.sum(p, axis=-1, keepdims=True)
                acc = jnp.dot(p.astype(BF16), v_sb, preferred_element_type=F32)
                o_h = acc / den
                lse_all = jnp.where(lane // LSE_LANES == 2 * pair + half,
                                    m + jnp.log(den) * LOG2E, lse_all)
                o_pair = o_h if half == 0 else jnp.where(lane < HEAD_DIM, o_pair, o_h)
            o_ref[g, rows, sl] = o_pair.astype(BF16)
            lse_ref[g, rows, :] = lse_all


def _band_attention(q, k, v, bias):
    s, n, _ = q.shape
    step = min(n, BAND_STEP)
    grp = BAND_STEP // step
    nq = n // step
    nk = n // BAND
    per = step // BAND
    cur = pl.BlockSpec((grp, step, WIDTH), lambda i, j: (i, j, 0))
    prev = pl.BlockSpec((grp, BAND, WIDTH), lambda i, j: (i, jnp.maximum(j * per - 1, 0), 0))
    nxt = pl.BlockSpec((grp, BAND, WIDTH), lambda i, j: (i, jnp.minimum((j + 1) * per, nk - 1), 0))
    return pl.pallas_call(
        _band_kernel,
        grid=(s // grp, nq),
        in_specs=[cur, prev, cur, nxt, prev, cur, nxt,
                  pl.BlockSpec((N_HEADS_A, BAND_TQ, BAND_TK), lambda i, j: (0, 0, 0))],
        out_specs=[cur, pl.BlockSpec((grp, step, LANES), lambda i, j: (i, j, 0))],
        out_shape=[jax.ShapeDtypeStruct((s, n, WIDTH), BF16),
                   jax.ShapeDtypeStruct((s, n, LANES), F32)],
        compiler_params=_params("arbitrary", "arbitrary"),
        name="band_attention",
    )(q, k, k, k, v, v, v, bias)


def _diff_kernel(lam_init, q0_ref, q1_ref, k0_ref, k1_ref, vt_ref, bias_ref,
                 lq1_ref, lk1_ref, lq2_ref, lk2_ref, g_ref, o_ref, a0_ref, a1_ref, s_ref):
    head = pl.program_id(1)
    qi = pl.program_id(2)
    n_kv = k0_ref.shape[1] // DIFF_TK
    row = lax.broadcasted_iota(jnp.int32, (LANES, 1), 0)
    mask = (row // HEAD_DIM == head % 2).astype(F32)
    q0t = (q0_ref[0, 0, 0].astype(F32) * mask).astype(BF16)
    q1t = (q1_ref[0, 0, 0].astype(F32) * mask).astype(BF16)
    a0_ref[...] = jnp.zeros(a0_ref.shape, F32)
    a1_ref[...] = jnp.zeros(a1_ref.shape, F32)
    maps = ((q0t, k0_ref, a0_ref), (q1t, k1_ref, a1_ref))

    def scores(kk, slot):
        start = pl.multiple_of(kk * DIFF_TK, DIFF_TK)
        off = kk * (DIFF_TK // DIFF_TQ) - qi
        bias = bias_ref[0, jnp.clip(off, DIFF_TILE_LO, DIFF_TILE_HI) - DIFF_TILE_LO]
        cmax = []
        for m, (qt, k_ref, _) in enumerate(maps):
            s = jnp.dot(k_ref[0, pl.ds(start, DIFF_TK), :], qt, preferred_element_type=F32) + bias
            s_ref[slot, m] = s
            cmax.append(jnp.max(s, axis=0, keepdims=True))
        return tuple(cmax)

    def consume(kk, slot, cmax, stats):
        vt = vt_ref[0, 0, kk]
        out = []
        for m, (_, _, a_ref) in enumerate(maps):
            m_new = jnp.maximum(stats[m], cmax[m])
            alpha = jnp.exp2(stats[m] - m_new)
            p = jnp.exp2(s_ref[slot, m] - m_new)
            a_ref[...] = alpha * a_ref[...] + jnp.dot(vt, p.astype(BF16),
                                                      preferred_element_type=F32)
            out.append(m_new)
        return tuple(out)

    def pair(j, carry):
        cmax, stats = carry
        nxt = scores(2 * j + 1, 1)
        stats = consume(2 * j, 0, cmax, stats)
        cmax = scores(2 * j + 2, 0)
        stats = consume(2 * j + 1, 1, nxt, stats)
        return cmax, stats

    stat = jnp.full((1, DIFF_TQ), NEG, F32)

    cmax, stats = lax.fori_loop(0, n_kv // 2 - 1, pair, (scores(0, 0), (stat, stat)))
    nxt = scores(n_kv - 1, 1)
    stats = consume(n_kv - 2, 0, cmax, stats)
    consume(n_kv - 1, 1, nxt, stats)

    lam = (jnp.exp(jnp.sum(lq1_ref[...] * lk1_ref[...], axis=-1, keepdims=True))
           - jnp.exp(jnp.sum(lq2_ref[...] * lk2_ref[...], axis=-1, keepdims=True))
           + lam_init)
    l0 = a0_ref[LANES:LANES + 1, :]
    l1 = a1_ref[LANES:LANES + 1, :]
    o = (a0_ref[0:LANES, :] / l0 - lam * (a1_ref[0:LANES, :] / l1)).T
    o_ref[0] = (_rms(o, g_ref[...]) * (1.0 - lam_init)).astype(BF16)


def _diff_attention(qbt, kb, vbt, bias, lq1, lk1, lq2, lk2, g, lam_init):
    b, l, _ = kb.shape
    nq = l // DIFF_TQ
    nk = l // DIFF_TK
    n_tiles = bias.shape[1]
    pairs = N_HEADS_B // 2
    qspec = lambda m: pl.BlockSpec((1, 1, 1, LANES, DIFF_TQ),
                                   lambda i, h, j: (i, m * pairs + h // 2, j, 0, 0))
    kspec = lambda m: pl.BlockSpec((1, l, LANES), lambda i, h, j: (i, 0, m * pairs + h // 2))
    small = pl.BlockSpec((1, HEAD_DIM), lambda i, h, j: (0, 0))
    acc = pltpu.VMEM((V_ROWS, DIFF_TQ), F32)
    return pl.pallas_call(
        functools.partial(_diff_kernel, lam_init),
        grid=(b, N_HEADS_B, nq),
        in_specs=[
            qspec(0), qspec(1), kspec(0), kspec(1),
            pl.BlockSpec((1, 1, nk, V_ROWS, DIFF_TK), lambda i, h, j: (i, h, 0, 0, 0)),
            pl.BlockSpec((1, n_tiles, DIFF_TK, DIFF_TQ), lambda i, h, j: (h, 0, 0, 0)),
            small, small, small, small,
            pl.BlockSpec((1, 2 * HEAD_DIM), lambda i, h, j: (0, 0)),
        ],
        out_specs=pl.BlockSpec((1, DIFF_TQ, LANES), lambda i, h, j: (i, j, h)),
        out_shape=jax.ShapeDtypeStruct((b, l, WIDTH), BF16),
        scratch_shapes=[acc, acc, pltpu.VMEM((2, 2, DIFF_TK, DIFF_TQ), F32)],
        compiler_params=_params("arbitrary", "arbitrary", "arbitrary"),
        name="diff_attention",
    )(qbt, qbt, kb, kb, vbt, bias, lq1, lk1, lq2, lk2, g)


def _outproj_kernel(x_ref, o1_ref, l1_ref, o4_ref, l4_ref, o16_ref, l16_ref, ob_ref,
                    mod_ref, w_ref, y_ref, so4_ref, sl4_ref, so16_ref, sl16_ref, oa_ref):
    tm = x_ref.shape[1]
    for d, src, dst in ((DILATIONS[1], l4_ref, sl4_ref), (DILATIONS[2], l16_ref, sl16_ref)):
        for r in range(d):
            dst[pl.ds(r, tm // d, stride=d), :] = src[0, r]
    l1, l2, l3 = l1_ref[0], sl4_ref[...], sl16_ref[...]
    mx = jnp.maximum(jnp.maximum(l1, l2), l3)
    e1, e2, e3 = jnp.exp2(l1 - mx), jnp.exp2(l2 - mx), jnp.exp2(l3 - mx)
    inv = 1.0 / (e1 + e2 + e3)
    w1, w2, w3 = e1 * inv, e2 * inv, e3 * inv
    lane = lax.broadcasted_iota(jnp.int32, (1, LANES), 1)
    for grp in range(WIDTH // LANES):
        lanes = slice(grp * LANES, (grp + 1) * LANES)
        for d, src, dst in ((DILATIONS[1], o4_ref, so4_ref), (DILATIONS[2], o16_ref, so16_ref)):
            for r in range(d):
                dst[grp, pl.ds(r, tm // d, stride=d), :] = src[0, r, :, lanes].astype(F32)
        lo, hi = 2 * grp * LSE_LANES, (2 * grp + 1) * LSE_LANES
        wide = lambda w: jnp.where(lane < HEAD_DIM, w[:, lo:lo + 1], w[:, hi:hi + 1])
        oa = (wide(w1) * o1_ref[0, :, lanes].astype(F32) + wide(w2) * so4_ref[grp]
              + wide(w3) * so16_ref[grp])
        oa_ref[:, lanes] = oa.astype(BF16)
    mix = (jnp.dot(oa_ref[...], w_ref[0:WIDTH, :], preferred_element_type=F32)
           + jnp.dot(ob_ref[0], w_ref[WIDTH:2 * WIDTH, :], preferred_element_type=F32))
    y_ref[0] = x_ref[0] + mod_ref[0, 2:3, :] * mix


def _outproj(x, band_out, ob, mod, w_out):
    b, l, _ = x.shape
    tm = PROJ_TILE
    xs = pl.BlockSpec((1, tm, D_MODEL), lambda i, j: (i, j, 0))
    hs = pl.BlockSpec((1, tm, WIDTH), lambda i, j: (i, j, 0))
    ls = pl.BlockSpec((1, tm, LANES), lambda i, j: (i, j, 0))
    sub = lambda d, w: pl.BlockSpec((1, d, tm // d, w), lambda i, j: (i, 0, j, 0))
    d4, d16 = DILATIONS[1], DILATIONS[2]
    stage_o = pltpu.VMEM((WIDTH // LANES, tm, LANES), F32)
    stage_l = pltpu.VMEM((tm, LANES), F32)
    (o1, l1), (o4, l4), (o16, l16) = band_out
    return pl.pallas_call(
        _outproj_kernel,
        grid=(b, l // tm),
        in_specs=[xs, hs, ls, sub(d4, WIDTH), sub(d4, LANES), sub(d16, WIDTH), sub(d16, LANES), hs,
                  pl.BlockSpec((1, 6, D_MODEL), lambda i, j: (i, 0, 0)),
                  pl.BlockSpec((2 * WIDTH, D_MODEL), lambda i, j: (0, 0))],
        out_specs=xs,
        out_shape=jax.ShapeDtypeStruct(x.shape, F32),
        scratch_shapes=[stage_o, stage_l, stage_o, stage_l, pltpu.VMEM((tm, WIDTH), BF16)],
        compiler_params=_params("arbitrary", "arbitrary"),
        name="outproj",
    )(x, o1, l1, o4, l4, o16, l16, ob, mod, w_out)


def _ffn_kernel(final, x_ref, mod_ref, g_ref, wg_ref, wu_ref, wd_ref, fg_ref, y_ref, act_ref):
    x = x_ref[0]
    h = _rms(x, g_ref[...]) * (1.0 + mod_ref[0, 4:5, :]) + mod_ref[0, 3:4, :]
    h = h.astype(BF16)
    for c in range(D_FF // FF_CHUNK):
        sl = slice(c * FF_CHUNK, (c + 1) * FF_CHUNK)
        gate = jnp.dot(h, wg_ref[:, sl], preferred_element_type=F32)
        up = jnp.dot(h, wu_ref[:, sl], preferred_element_type=F32)
        act_ref[:, sl] = (gate * jax.nn.sigmoid(gate) * up).astype(BF16)
    ff = jnp.dot(act_ref[...], wd_ref[...], preferred_element_type=F32)
    y = x + mod_ref[0, 5:6, :] * ff
    if final:
        y = _rms(y, fg_ref[...])
    y_ref[0] = y


def _ffn(x, mod, g, w_gate, w_up, w_down, final_g, final):
    b, l, _ = x.shape
    tm = FFN_TILE
    xs = pl.BlockSpec((1, tm, D_MODEL), lambda i, j: (i, j, 0))
    vec = pl.BlockSpec((1, D_MODEL), lambda i, j: (0, 0))
    once = pl.Buffered(1)
    return pl.pallas_call(
        functools.partial(_ffn_kernel, final),
        grid=(b, l // tm),
        in_specs=[xs, pl.BlockSpec((1, 6, D_MODEL), lambda i, j: (i, 0, 0)), vec,
                  pl.BlockSpec((D_MODEL, D_FF), lambda i, j: (0, 0), pipeline_mode=once),
                  pl.BlockSpec((D_MODEL, D_FF), lambda i, j: (0, 0), pipeline_mode=once),
                  pl.BlockSpec((D_FF, D_MODEL), lambda i, j: (0, 0), pipeline_mode=once),
                  vec],
        out_specs=xs,
        out_shape=jax.ShapeDtypeStruct(x.shape, F32),
        scratch_shapes=[pltpu.VMEM((tm, D_FF), BF16)],
        compiler_params=_params("arbitrary", "arbitrary"),
        name="ffn",
    )(x, mod, g, w_gate, w_up, w_down, final_g)


def _trunk(x, mods, band_bias, diff_bias, p):
    b, l, _ = x.shape
    for layer in range(DEPTH):
        mod = mods[layer]
        outs = _inproj(x, mod, p["norm_mix_g"][layer:layer + 1], p["w_nat"][layer], p["w_t"][layer])
        kb, qbt, vbt = outs[9:12]
        band_out = []
        for i, d in enumerate(DILATIONS):
            q, k, v = (t.reshape(b * d, l // d, WIDTH) for t in outs[3 * i:3 * i + 3])
            o, lse = _band_attention(q, k, v, band_bias[i])
            lead = (b, l) if d == 1 else (b, d, l // d)
            band_out.append((o.reshape(lead + (WIDTH,)), lse.reshape(lead + (LANES,))))
        lam_init = 0.8 - 0.6 * math.exp(-0.3 * layer)
        ob = _diff_attention(qbt, kb, vbt, diff_bias,
                             p["lambda_q1"][layer:layer + 1], p["lambda_k1"][layer:layer + 1],
                             p["lambda_q2"][layer:layer + 1], p["lambda_k2"][layer:layer + 1],
                             p["subln_g"][layer:layer + 1], lam_init)
        x = _outproj(x, band_out, ob, mod, p["w_out"][layer])
        x = _ffn(x, mod, p["norm_ffn_g"][layer:layer + 1], p["w_gate"][layer], p["w_up"][layer],
                 p["w_down"][layer], p["final_g"], final=(layer == DEPTH - 1))
    return x


def kernel(x_prompt, x_sample, c_prompt, c_sample, rel_bias, ada_w, ada_b, norm_mix_g,
           norm_ffn_g, w_in, w_out, lambda_q1, lambda_k1, lambda_q2, lambda_k2, subln_g,
           w_gate, w_up, w_down, final_g):
    nb_p, nb_s = c_prompt.shape[0], c_sample.shape[0]
    rows = -(-(nb_p + nb_s) // 8) * 8
    c_all = jnp.zeros((rows, D_MODEL), F32).at[:nb_p].set(c_prompt).at[nb_p:nb_p + nb_s].set(c_sample)
    mods = _modulation(c_all, ada_w, ada_b).reshape(DEPTH, rows, 6, D_MODEL)
    band_bias = _band_bias_tiles(rel_bias[:, :N_HEADS_A])
    diff_bias = _diff_bias_tiles(rel_bias[:, N_HEADS_A:])
    w_bf = w_in.astype(BF16)
    w_nat = jnp.concatenate([w_bf[:, :, :3 * WIDTH], w_bf[:, :, 4 * WIDTH:5 * WIDTH]], axis=2)
    w_t = jnp.concatenate([w_bf[:, :, 3 * WIDTH:4 * WIDTH], w_bf[:, :, 5 * WIDTH:]],
                          axis=2).transpose(0, 2, 1)
    p = dict(norm_mix_g=norm_mix_g, norm_ffn_g=norm_ffn_g, w_nat=w_nat, w_t=w_t,
             w_out=w_out.astype(BF16), lambda_q1=lambda_q1, lambda_k1=lambda_k1,
             lambda_q2=lambda_q2, lambda_k2=lambda_k2, subln_g=subln_g,
             w_gate=w_gate.astype(BF16), w_up=w_up.astype(BF16), w_down=w_down.astype(BF16),
             final_g=final_g.reshape(1, D_MODEL))
    y_prompt = _trunk(x_prompt, mods[:, :nb_p], band_bias, diff_bias, p)
    y_sample = _trunk(x_sample, mods[:, nb_p:nb_p + nb_s], band_bias, diff_bias, p)
    return (y_prompt, y_sample)
```

```python
import functools
import math

import jax
import jax.numpy as jnp
import numpy as np
from jax import lax
from jax.experimental import pallas as pl
from jax.experimental.pallas import tpu as pltpu

D_MODEL = 1024
HEAD_DIM = 64
N_HEADS_A = 8
N_HEADS_B = 4
WIDTH = 512
D_FF = 2816
DEPTH = 2
PATTERNS = ((128, 1), (512, 4), (2048, 16))
DILATIONS = tuple(d for _, d in PATTERNS)
BAND = 64
N_BUCKETS = 32
MAX_DISTANCE = 1024
EPS = 1e-6
NEG = -1e30
LOG2E = 1.4426950408889634
Q_SCALE = LOG2E * HEAD_DIM ** -0.5

LANES = 128
BAND_TQ = 128
BAND_TK = BAND_TQ + 2 * BAND
BAND_STEP = 512
LSE_LANES = LANES // N_HEADS_A
DIFF_TQ = 1024
DIFF_TK = 512
DIFF_UNIT = math.gcd(DIFF_TQ, DIFF_TK)
DIFF_TILE_LO = -3
DIFF_TILE_HI = 4
V_ROWS = LANES + 16
PROJ_TILE = 512
FF_CHUNK = 256
VMEM_LIMIT = 56 * 1024 * 1024

F32 = jnp.float32
BF16 = jnp.bfloat16
NT_DIMS = (((1,), (1,)), ((), ()))


def _params(*sem):
    return pltpu.CompilerParams(dimension_semantics=sem, vmem_limit_bytes=VMEM_LIMIT)


def _t5_bucket_np(rel):
    nb = N_BUCKETS // 2
    max_exact = nb // 2
    ret = (rel > 0).astype(np.int32) * nb
    n = np.abs(rel)
    nf = np.maximum(n, 1).astype(np.float32)
    large = max_exact + (np.log(nf / np.float32(max_exact))
                         / np.float32(math.log(MAX_DISTANCE / max_exact))
                         * np.float32(nb - max_exact)).astype(np.int32)
    large = np.minimum(large, nb - 1)
    return ret + np.where(n < max_exact, n, large)


def _toeplitz_offsets(rows, cols):
    t = np.arange(rows + cols)
    return np.where(t < cols, t, t - (rows + cols))


def _toeplitz_kernel(v_ref, o_ref):
    rows, cols = o_ref.shape[1:]
    x = jnp.broadcast_to(v_ref[0], (rows, rows + cols))
    o_ref[0] = pltpu.roll(x, 0, 1, stride=1, stride_axis=0)[:, :cols]


def _toeplitz(vecs, rows, cols):
    p = rows + cols
    lead = vecs.shape[:-1]
    n = math.prod(lead)
    out = pl.pallas_call(
        _toeplitz_kernel,
        grid=(n,),
        in_specs=[pl.BlockSpec((1, 1, p), lambda i: (i, 0, 0))],
        out_specs=pl.BlockSpec((1, rows, cols), lambda i: (i, 0, 0)),
        out_shape=jax.ShapeDtypeStruct((n, rows, cols), F32),
        compiler_params=_params("arbitrary"),
        name="toeplitz_bias",
    )(vecs.reshape(n, 1, p))
    return out.reshape(lead + (rows, cols))


def _band_bias_tiles(table_a):
    vecs = []
    for d in DILATIONS:
        delta = _toeplitz_offsets(BAND_TQ, BAND_TK) - BAND
        idx = _t5_bucket_np(delta * d)
        vecs.append(jnp.where(jnp.asarray(np.abs(delta) <= BAND)[None, :],
                              table_a.T[:, idx] * LOG2E, NEG))
    return _toeplitz(jnp.stack(vecs), BAND_TQ, BAND_TK)


def _diff_bias_tiles(table_b):
    vecs = []
    for t in range(DIFF_TILE_LO, DIFF_TILE_HI + 1):
        delta = t * DIFF_UNIT - _toeplitz_offsets(DIFF_TK, DIFF_TQ)
        vecs.append(table_b.T[:, _t5_bucket_np(delta)] * LOG2E)
    return _toeplitz(jnp.stack(vecs, axis=1), DIFF_TK, DIFF_TQ)


def _mod_kernel(c_ref, w_ref, b_ref, o_ref):
    c = c_ref[...]
    s = c * jax.nn.sigmoid(c)
    o_ref[0] = jnp.dot(s, w_ref[0], precision=lax.Precision.HIGHEST,
                       preferred_element_type=F32) + b_ref[0]


def _modulation(c_all, ada_w, ada_b):
    rows = c_all.shape[0]
    n_col = 6
    return pl.pallas_call(
        _mod_kernel,
        grid=(DEPTH, n_col),
        in_specs=[
            pl.BlockSpec((rows, D_MODEL), lambda l, j: (0, 0)),
            pl.BlockSpec((1, D_MODEL, D_MODEL), lambda l, j: (l, 0, j)),
            pl.BlockSpec((1, 1, D_MODEL), lambda l, j: (l, 0, j)),
        ],
        out_specs=pl.BlockSpec((1, rows, D_MODEL), lambda l, j: (l, 0, j)),
        out_shape=jax.ShapeDtypeStruct((DEPTH, rows, n_col * D_MODEL), F32),
        compiler_params=_params("arbitrary", "arbitrary"),
        name="adaln_modulation",
    )(c_all, ada_w, ada_b.reshape(DEPTH, 1, n_col * D_MODEL))


def _rms(x, g):
    return x * lax.rsqrt(jnp.mean(x * x, axis=-1, keepdims=True) + EPS) * g


def _inproj_kernel(x_ref, mod_ref, g_ref, wn_ref, wt_ref, *refs):
    nat, sub4, sub16 = refs[0:3], refs[3:6], refs[6:9]
    kb_ref, qbt_ref, vbt_ref, stage_ref = refs[9:13]
    tm = x_ref.shape[1]
    x = x_ref[0]
    h = _rms(x, g_ref[...]) * (1.0 + mod_ref[0, 1:2, :]) + mod_ref[0, 0:1, :]
    h = h.astype(BF16)
    for i in range(3):
        p = jnp.dot(h, wn_ref[:, i * WIDTH:(i + 1) * WIDTH], preferred_element_type=F32)
        if i == 0:
            p = p * Q_SCALE
        nat[i][0] = p.astype(BF16)
        for grp in range(WIDTH // LANES):
            lanes = slice(grp * LANES, (grp + 1) * LANES)
            stage_ref[grp] = p[:, lanes]
            for d, subs in ((DILATIONS[1], sub4), (DILATIONS[2], sub16)):
                for r in range(d):
                    subs[i][0, r, :, lanes] = stage_ref[grp, pl.ds(r, tm // d, stride=d),
                                                        :].astype(BF16)
    kb_ref[0] = jnp.dot(h, wn_ref[:, 3 * WIDTH:4 * WIDTH],
                        preferred_element_type=F32).astype(BF16)
    qt = lax.dot_general(wt_ref[0:WIDTH, :], h, NT_DIMS, preferred_element_type=F32) * Q_SCALE
    for grp in range(WIDTH // LANES):
        width = qbt_ref.shape[-1]
        for c in range(tm // width):
            qbt_ref[0, grp, c] = qt[grp * LANES:(grp + 1) * LANES,
                                    c * width:(c + 1) * width].astype(BF16)
    vt = lax.dot_general(wt_ref[WIDTH:2 * WIDTH, :], h, NT_DIMS, preferred_element_type=F32)
    for grp in range(N_HEADS_B):
        for c in range(tm // DIFF_TK):
            vbt_ref[0, grp, c, 0:LANES, :] = vt[grp * LANES:(grp + 1) * LANES,
                                                c * DIFF_TK:(c + 1) * DIFF_TK].astype(BF16)
            vbt_ref[0, grp, c, LANES:V_ROWS, :] = jnp.ones((V_ROWS - LANES, DIFF_TK), BF16)


def _inproj(x, mod, g, w_nat, w_t):
    b, l, _ = x.shape
    tm = PROJ_TILE
    groups = WIDTH // LANES
    nat = jax.ShapeDtypeStruct((b, l, WIDTH), BF16)
    nat_spec = pl.BlockSpec((1, tm, WIDTH), lambda i, j: (i, j, 0))
    sub = lambda d: jax.ShapeDtypeStruct((b, d, l // d, WIDTH), BF16)
    sub_spec = lambda d: pl.BlockSpec((1, d, tm // d, WIDTH), lambda i, j: (i, 0, j, 0))
    d4, d16 = DILATIONS[1], DILATIONS[2]
    return pl.pallas_call(
        _inproj_kernel,
        grid=(b, l // tm),
        in_specs=[
            pl.BlockSpec((1, tm, D_MODEL), lambda i, j: (i, j, 0)),
            pl.BlockSpec((1, 6, D_MODEL), lambda i, j: (i, 0, 0)),
            pl.BlockSpec((1, D_MODEL), lambda i, j: (0, 0)),
            pl.BlockSpec((D_MODEL, 4 * WIDTH), lambda i, j: (0, 0)),
            pl.BlockSpec((2 * WIDTH, D_MODEL), lambda i, j: (0, 0)),
        ],
        out_specs=[nat_spec] * 3 + [sub_spec(d4)] * 3 + [sub_spec(d16)] * 3 + [
            nat_spec,
            pl.BlockSpec((1, groups, max(tm // DIFF_TQ, 1), LANES, min(tm, DIFF_TQ)),
                         lambda i, j: (i, 0, j * tm // DIFF_TQ, 0, j % max(DIFF_TQ // tm, 1))),
            pl.BlockSpec((1, N_HEADS_B, tm // DIFF_TK, V_ROWS, DIFF_TK), lambda i, j: (i, 0, j, 0, 0)),
        ],
        out_shape=[nat] * 3 + [sub(d4)] * 3 + [sub(d16)] * 3 + [
            nat,
            jax.ShapeDtypeStruct((b, groups, l // DIFF_TQ, LANES, DIFF_TQ), BF16),
            jax.ShapeDtypeStruct((b, N_HEADS_B, l // DIFF_TK, V_ROWS, DIFF_TK), BF16),
        ],
        scratch_shapes=[pltpu.VMEM((groups, tm, LANES), F32)],
        compiler_params=_params("arbitrary", "arbitrary"),
        name="inproj",
    )(x, mod, g, w_nat, w_t)


def _lane_half_mask(half):
    lane = lax.broadcasted_iota(jnp.int32, (1, LANES), 1)
    return jnp.where(lane // HEAD_DIM == half, 1.0, 0.0).astype(BF16)


def _band_kernel(q_ref, kp_ref, kc_ref, kn_ref, vp_ref, vc_ref, vn_ref, bias_ref,
                 o_ref, lse_ref):
    j = pl.program_id(1)
    last = pl.num_programs(1) - 1
    n_sub = q_ref.shape[1] // BAND_TQ
    col = lax.broadcasted_iota(jnp.int32, (1, BAND_TK), 1)
    lo_edge = jnp.where(col < BAND, jnp.where(j == 0, NEG, 0.0), 0.0)
    hi_edge = jnp.where(col >= BAND + BAND_TQ, jnp.where(j == last, NEG, 0.0), 0.0)
    lane = lax.broadcasted_iota(jnp.int32, (1, LANES), 1)
    for g, pair in [(g, pair) for g in range(q_ref.shape[0]) for pair in range(N_HEADS_A // 2)]:
        sl = slice(pair * LANES, (pair + 1) * LANES)
        k = jnp.concatenate([kp_ref[g, :, sl], kc_ref[g, :, sl], kn_ref[g, :, sl]], axis=0)
        v = jnp.concatenate([vp_ref[g, :, sl], vc_ref[g, :, sl], vn_ref[g, :, sl]], axis=0)
        for sb in range(n_sub):
            rows = slice(sb * BAND_TQ, (sb + 1) * BAND_TQ)
            q = q_ref[g, rows, sl]
            k_sb = k[sb * BAND_TQ:sb * BAND_TQ + BAND_TK]
            v_sb = v[sb * BAND_TQ:sb * BAND_TQ + BAND_TK]
            o_pair = None
            lse_all = lse_ref[g, rows, :] if pair else jnp.zeros((BAND_TQ, LANES), F32)
            for half in range(2):
                qm = q * _lane_half_mask(half)
                s = lax.dot_general(qm, k_sb, NT_DIMS, preferred_element_type=F32)
                s = s + bias_ref[2 * pair + half]
                if sb == 0:
                    s = s + lo_edge
                if sb == n_sub - 1:
                    s = s + hi_edge
                m = jnp.max(s, axis=-1, keepdims=True)
                p = jnp.exp2(s - m)
                den = jnp.sum(p, axis=-1, keepdims=True)
                acc = jnp.dot(p.astype(BF16), v_sb, preferred_element_type=F32)
                o_h = acc / den
                lse_all = jnp.where(lane // LSE_LANES == 2 * pair + half,
                                    m + jnp.log(den) * LOG2E, lse_all)
                o_pair = o_h if half == 0 else jnp.where(lane < HEAD_DIM, o_pair, o_h)
            o_ref[g, rows, sl] = o_pair.astype(BF16)
            lse_ref[g, rows, :] = lse_all


def _band_attention(q, k, v, bias):
    s, n, _ = q.shape
    step = min(n, BAND_STEP)
    grp = BAND_STEP // step
    nq = n // step
    nk = n // BAND
    per = step // BAND
    cur = pl.BlockSpec((grp, step, WIDTH), lambda i, j: (i, j, 0))
    prev = pl.BlockSpec((grp, BAND, WIDTH), lambda i, j: (i, jnp.maximum(j * per - 1, 0), 0))
    nxt = pl.BlockSpec((grp, BAND, WIDTH), lambda i, j: (i, jnp.minimum((j + 1) * per, nk - 1), 0))
    return pl.pallas_call(
        _band_kernel,
        grid=(s // grp, nq),
        in_specs=[cur, prev, cur, nxt, prev, cur, nxt,
                  pl.BlockSpec((N_HEADS_A, BAND_TQ, BAND_TK), lambda i, j: (0, 0, 0))],
        out_specs=[cur, pl.BlockSpec((grp, step, LANES), lambda i, j: (i, j, 0))],
        out_shape=[jax.ShapeDtypeStruct((s, n, WIDTH), BF16),
                   jax.ShapeDtypeStruct((s, n, LANES), F32)],
        compiler_params=_params("arbitrary", "arbitrary"),
        name="band_attention",
    )(q, k, k, k, v, v, v, bias)


def _diff_kernel(lam_init, q0_ref, q1_ref, k0_ref, k1_ref, vt_ref, bias_ref,
                 lq1_ref, lk1_ref, lq2_ref, lk2_ref, g_ref, o_ref, a0_ref, a1_ref, s_ref):
    head = pl.program_id(0)
    qi = pl.program_id(2)
    n_kv = k0_ref.shape[1] // DIFF_TK
    row = lax.broadcasted_iota(jnp.int32, (LANES, 1), 0)
    mask = (row // HEAD_DIM == head % 2).astype(F32)
    q0t = (q0_ref[0, 0, 0].astype(F32) * mask).astype(BF16)
    q1t = (q1_ref[0, 0, 0].astype(F32) * mask).astype(BF16)
    a0_ref[...] = jnp.zeros(a0_ref.shape, F32)
    a1_ref[...] = jnp.zeros(a1_ref.shape, F32)
    maps = ((q0t, k0_ref, a0_ref), (q1t, k1_ref, a1_ref))

    def scores(kk, slot):
        start = pl.multiple_of(kk * DIFF_TK, DIFF_TK)
        off = kk * (DIFF_TK // DIFF_UNIT) - qi * (DIFF_TQ // DIFF_UNIT)
        bias = bias_ref[0, jnp.clip(off, DIFF_TILE_LO, DIFF_TILE_HI) - DIFF_TILE_LO]
        cmax = []
        for m, (qt, k_ref, _) in enumerate(maps):
            s = jnp.dot(k_ref[0, pl.ds(start, DIFF_TK), :], qt, preferred_element_type=F32) + bias
            s_ref[slot, m] = s
            cmax.append(jnp.max(s, axis=0, keepdims=True))
        return tuple(cmax)

    def consume(kk, slot, cmax, stats):
        vt = vt_ref[0, 0, kk]
        out = []
        for m, (_, _, a_ref) in enumerate(maps):
            m_new = jnp.maximum(stats[m], cmax[m])
            alpha = jnp.exp2(stats[m] - m_new)
            p = jnp.exp2(s_ref[slot, m] - m_new)
            a_ref[...] = alpha * a_ref[...] + jnp.dot(vt, p.astype(BF16),
                                                      preferred_element_type=F32)
            out.append(m_new)
        return tuple(out)

    def pair(j, carry):
        cmax, stats = carry
        nxt = scores(2 * j + 1, 1)
        stats = consume(2 * j, 0, cmax, stats)
        cmax = scores(2 * j + 2, 0)
        stats = consume(2 * j + 1, 1, nxt, stats)
        return cmax, stats

    stat = jnp.full((1, DIFF_TQ), NEG, F32)

    cmax, stats = lax.fori_loop(0, n_kv // 2 - 1, pair, (scores(0, 0), (stat, stat)))
    nxt = scores(n_kv - 1, 1)
    stats = consume(n_kv - 2, 0, cmax, stats)
    consume(n_kv - 1, 1, nxt, stats)

    lam = (jnp.exp(jnp.sum(lq1_ref[...] * lk1_ref[...], axis=-1, keepdims=True))
           - jnp.exp(jnp.sum(lq2_ref[...] * lk2_ref[...], axis=-1, keepdims=True))
           + lam_init)
    l0 = a0_ref[LANES:LANES + 1, :]
    l1 = a1_ref[LANES:LANES + 1, :]
    o = (a0_ref[0:LANES, :] / l0 - lam * (a1_ref[0:LANES, :] / l1)).T
    o_ref[0] = (_rms(o, g_ref[...]) * (1.0 - lam_init)).astype(BF16)


def _diff_attention(qbt, kb, vbt, bias, lq1, lk1, lq2, lk2, g, lam_init):
    b, l, _ = kb.shape
    nq = l // DIFF_TQ
    nk = l // DIFF_TK
    n_tiles = bias.shape[1]
    pairs = N_HEADS_B // 2
    qspec = lambda m: pl.BlockSpec((1, 1, 1, LANES, DIFF_TQ),
                                   lambda h, i, j: (i, m * pairs + h // 2, j, 0, 0))
    kspec = lambda m: pl.BlockSpec((1, l, LANES), lambda h, i, j: (i, 0, m * pairs + h // 2))
    small = pl.BlockSpec((1, HEAD_DIM), lambda h, i, j: (0, 0))
    acc = pltpu.VMEM((V_ROWS, DIFF_TQ), F32)
    return pl.pallas_call(
        functools.partial(_diff_kernel, lam_init),
        grid=(N_HEADS_B, b, nq),
        in_specs=[
            qspec(0), qspec(1), kspec(0), kspec(1),
            pl.BlockSpec((1, 1, nk, V_ROWS, DIFF_TK), lambda h, i, j: (i, h, 0, 0, 0)),
            pl.BlockSpec((1, n_tiles, DIFF_TK, DIFF_TQ), lambda h, i, j: (h, 0, 0, 0),
                         pipeline_mode=pl.Buffered(1)),
            small, small, small, small,
            pl.BlockSpec((1, 2 * HEAD_DIM), lambda h, i, j: (0, 0)),
        ],
        out_specs=pl.BlockSpec((1, DIFF_TQ, LANES), lambda h, i, j: (i, j, h)),
        out_shape=jax.ShapeDtypeStruct((b, l, WIDTH), BF16),
        scratch_shapes=[acc, acc, pltpu.VMEM((2, 2, DIFF_TK, DIFF_TQ), F32)],
        compiler_params=_params("arbitrary", "arbitrary", "arbitrary"),
        name="diff_attention",
    )(qbt, qbt, kb, kb, vbt, bias, lq1, lk1, lq2, lk2, g)


def _mix_residual(x_ref, o1_ref, l1_ref, o4_ref, l4_ref, o16_ref, l16_ref, ob_ref,
                  mod_ref, w_ref, so4_ref, sl4_ref, so16_ref, sl16_ref, oa_ref):
    tm = x_ref.shape[1]
    for d, src, dst in ((DILATIONS[1], l4_ref, sl4_ref), (DILATIONS[2], l16_ref, sl16_ref)):
        for r in range(d):
            dst[pl.ds(r, tm // d, stride=d), :] = src[0, r]
    l1, l2, l3 = l1_ref[0], sl4_ref[...], sl16_ref[...]
    mx = jnp.maximum(jnp.maximum(l1, l2), l3)
    e1, e2, e3 = jnp.exp2(l1 - mx), jnp.exp2(l2 - mx), jnp.exp2(l3 - mx)
    inv = 1.0 / (e1 + e2 + e3)
    w1, w2, w3 = e1 * inv, e2 * inv, e3 * inv
    lane = lax.broadcasted_iota(jnp.int32, (1, LANES), 1)
    for grp in range(WIDTH // LANES):
        lanes = slice(grp * LANES, (grp + 1) * LANES)
        for d, src, dst in ((DILATIONS[1], o4_ref, so4_ref), (DILATIONS[2], o16_ref, so16_ref)):
            for r in range(d):
                dst[grp, pl.ds(r, tm // d, stride=d), :] = src[0, r, :, lanes].astype(F32)
        lo, hi = 2 * grp * LSE_LANES, (2 * grp + 1) * LSE_LANES
        wide = lambda w: jnp.where(lane < HEAD_DIM, w[:, lo:lo + 1], w[:, hi:hi + 1])
        oa = (wide(w1) * o1_ref[0, :, lanes].astype(F32) + wide(w2) * so4_ref[grp]
              + wide(w3) * so16_ref[grp])
        oa_ref[:, lanes] = oa.astype(BF16)
    mix = (jnp.dot(oa_ref[...], w_ref[0:WIDTH, :], preferred_element_type=F32)
           + jnp.dot(ob_ref[0], w_ref[WIDTH:2 * WIDTH, :], preferred_element_type=F32))
    return x_ref[0] + mod_ref[0, 2:3, :] * mix


def _ffn_residual(final, x, mod_ref, g_ref, wg_ref, wu_ref, wd_ref, fg_ref, act_ref):
    h = _rms(x, g_ref[...]) * (1.0 + mod_ref[0, 4:5, :]) + mod_ref[0, 3:4, :]
    h = h.astype(BF16)
    for c in range(D_FF // FF_CHUNK):
        sl = slice(c * FF_CHUNK, (c + 1) * FF_CHUNK)
        gate = jnp.dot(h, wg_ref[:, sl], preferred_element_type=F32)
        up = jnp.dot(h, wu_ref[:, sl], preferred_element_type=F32)
        act_ref[:, sl] = (gate * jax.nn.sigmoid(gate) * up).astype(BF16)
    ff = jnp.dot(act_ref[...], wd_ref[...], preferred_element_type=F32)
    y = x + mod_ref[0, 5:6, :] * ff
    return _rms(y, fg_ref[...]) if final else y


def _mix_ffn_kernel(final, x_ref, o1_ref, l1_ref, o4_ref, l4_ref, o16_ref, l16_ref, ob_ref,
                    mod_ref, w_ref, g_ref, wg_ref, wu_ref, wd_ref, fg_ref, y_ref,
                    so4_ref, sl4_ref, so16_ref, sl16_ref, oa_ref, act_ref):
    x1 = _mix_residual(x_ref, o1_ref, l1_ref, o4_ref, l4_ref, o16_ref, l16_ref, ob_ref,
                       mod_ref, w_ref, so4_ref, sl4_ref, so16_ref, sl16_ref, oa_ref)
    y_ref[0] = _ffn_residual(final, x1, mod_ref, g_ref, wg_ref, wu_ref, wd_ref, fg_ref, act_ref)


def _mix_ffn(x, band_out, ob, mod, w_out, g, w_gate, w_up, w_down, final_g, final):
    b, l, _ = x.shape
    tm = PROJ_TILE
    xs = pl.BlockSpec((1, tm, D_MODEL), lambda i, j: (i, j, 0))
    hs = pl.BlockSpec((1, tm, WIDTH), lambda i, j: (i, j, 0))
    ls = pl.BlockSpec((1, tm, LANES), lambda i, j: (i, j, 0))
    sub = lambda d, w: pl.BlockSpec((1, d, tm // d, w), lambda i, j: (i, 0, j, 0))
    vec = pl.BlockSpec((1, D_MODEL), lambda i, j: (0, 0))
    once = lambda shape: pl.BlockSpec(shape, lambda i, j: (0, 0), pipeline_mode=pl.Buffered(1))
    d4, d16 = DILATIONS[1], DILATIONS[2]
    stage_o = pltpu.VMEM((WIDTH // LANES, tm, LANES), F32)
    stage_l = pltpu.VMEM((tm, LANES), F32)
    (o1, l1), (o4, l4), (o16, l16) = band_out
    return pl.pallas_call(
        functools.partial(_mix_ffn_kernel, final),
        grid=(b, l // tm),
        in_specs=[xs, hs, ls, sub(d4, WIDTH), sub(d4, LANES), sub(d16, WIDTH), sub(d16, LANES), hs,
                  pl.BlockSpec((1, 6, D_MODEL), lambda i, j: (i, 0, 0)),
                  once((2 * WIDTH, D_MODEL)), vec,
                  once((D_MODEL, D_FF)), once((D_MODEL, D_FF)), once((D_FF, D_MODEL)), vec],
        out_specs=xs,
        out_shape=jax.ShapeDtypeStruct(x.shape, F32),
        scratch_shapes=[stage_o, stage_l, stage_o, stage_l, pltpu.VMEM((tm, WIDTH), BF16),
                        pltpu.VMEM((tm, D_FF), BF16)],
        compiler_params=_params("arbitrary", "arbitrary"),
        name="mix_ffn",
    )(x, o1, l1, o4, l4, o16, l16, ob, mod, w_out, g, w_gate, w_up, w_down, final_g)


def _trunk(x, mods, band_bias, diff_bias, p):
    b, l, _ = x.shape
    for layer in range(DEPTH):
        mod = mods[layer]
        outs = _inproj(x, mod, p["norm_mix_g"][layer:layer + 1], p["w_nat"][layer], p["w_t"][layer])
        kb, qbt, vbt = outs[9:12]
        band_out = []
        for i, d in enumerate(DILATIONS):
            q, k, v = (t.reshape(b * d, l // d, WIDTH) for t in outs[3 * i:3 * i + 3])
            o, lse = _band_attention(q, k, v, band_bias[i])
            lead = (b, l) if d == 1 else (b, d, l // d)
            band_out.append((o.reshape(lead + (WIDTH,)), lse.reshape(lead + (LANES,))))
        lam_init = 0.8 - 0.6 * math.exp(-0.3 * layer)
        ob = _diff_attention(qbt, kb, vbt, diff_bias,
                             p["lambda_q1"][layer:layer + 1], p["lambda_k1"][layer:layer + 1],
                             p["lambda_q2"][layer:layer + 1], p["lambda_k2"][layer:layer + 1],
                             p["subln_g"][layer:layer + 1], lam_init)
        x = _mix_ffn(x, band_out, ob, mod, p["w_out"][layer], p["norm_ffn_g"][layer:layer + 1],
                     p["w_gate"][layer], p["w_up"][layer], p["w_down"][layer], p["final_g"],
                     final=(layer == DEPTH - 1))
    return x


def kernel(x_prompt, x_sample, c_prompt, c_sample, rel_bias, ada_w, ada_b, norm_mix_g,
           norm_ffn_g, w_in, w_out, lambda_q1, lambda_k1, lambda_q2, lambda_k2, subln_g,
           w_gate, w_up, w_down, final_g):
    nb_p, nb_s = c_prompt.shape[0], c_sample.shape[0]
    rows = -(-(nb_p + nb_s) // 8) * 8
    c_all = jnp.zeros((rows, D_MODEL), F32).at[:nb_p].set(c_prompt).at[nb_p:nb_p + nb_s].set(c_sample)
    mods = _modulation(c_all, ada_w, ada_b).reshape(DEPTH, rows, 6, D_MODEL)
    band_bias = _band_bias_tiles(rel_bias[:, :N_HEADS_A])
    diff_bias = _diff_bias_tiles(rel_bias[:, N_HEADS_A:])
    w_bf = w_in.astype(BF16)
    w_nat = jnp.concatenate([w_bf[:, :, :3 * WIDTH], w_bf[:, :, 4 * WIDTH:5 * WIDTH]], axis=2)
    w_t = jnp.concatenate([w_bf[:, :, 3 * WIDTH:4 * WIDTH], w_bf[:, :, 5 * WIDTH:]],
                          axis=2).transpose(0, 2, 1)
    p = dict(norm_mix_g=norm_mix_g, norm_ffn_g=norm_ffn_g, w_nat=w_nat, w_t=w_t,
             w_out=w_out.astype(BF16), lambda_q1=lambda_q1, lambda_k1=lambda_k1,
             lambda_q2=lambda_q2, lambda_k2=lambda_k2, subln_g=subln_g,
             w_gate=w_gate.astype(BF16), w_up=w_up.astype(BF16), w_down=w_down.astype(BF16),
             final_g=final_g.reshape(1, D_MODEL))
    y_prompt = _trunk(x_prompt, mods[:, :nb_p], band_bias, diff_bias, p)
    y_sample = _trunk(x_sample, mods[:, nb_p:nb_p + nb_s], band_bias, diff_bias, p)
    return (y_prompt, y_sample)
```

```python
import functools
import math

import jax
import jax.numpy as jnp
import numpy as np
from jax import lax
from jax.experimental import pallas as pl
from jax.experimental.pallas import tpu as pltpu

D_MODEL = 1024
HEAD_DIM = 64
N_HEADS_A = 8
N_HEADS_B = 4
WIDTH = 512
D_FF = 2816
DEPTH = 2
PATTERNS = ((128, 1), (512, 4), (2048, 16))
DILATIONS = tuple(d for _, d in PATTERNS)
BAND = 64
N_BUCKETS = 32
MAX_DISTANCE = 1024
EPS = 1e-6
NEG = -1e30
LOG2E = 1.4426950408889634
Q_SCALE = LOG2E * HEAD_DIM ** -0.5

LANES = 128
BAND_TQ = 128
BAND_TK = BAND_TQ + 2 * BAND
BAND_STEP = 2048
LSE_LANES = LANES // N_HEADS_A
DIFF_TQ = 1024
DIFF_TK = 512
DIFF_UNIT = math.gcd(DIFF_TQ, DIFF_TK)
DIFF_TILE_LO = -3
DIFF_TILE_HI = 4
V_ROWS = LANES + 16
PROJ_TILE = 512
FF_CHUNK = 256
VMEM_LIMIT = 56 * 1024 * 1024

F32 = jnp.float32
BF16 = jnp.bfloat16
NT_DIMS = (((1,), (1,)), ((), ()))


def _params(*sem):
    return pltpu.CompilerParams(dimension_semantics=sem, vmem_limit_bytes=VMEM_LIMIT)


def _t5_bucket_np(rel):
    nb = N_BUCKETS // 2
    max_exact = nb // 2
    ret = (rel > 0).astype(np.int32) * nb
    n = np.abs(rel)
    nf = np.maximum(n, 1).astype(np.float32)
    large = max_exact + (np.log(nf / np.float32(max_exact))
                         / np.float32(math.log(MAX_DISTANCE / max_exact))
                         * np.float32(nb - max_exact)).astype(np.int32)
    large = np.minimum(large, nb - 1)
    return ret + np.where(n < max_exact, n, large)


def _toeplitz_offsets(rows, cols):
    t = np.arange(rows + cols)
    return np.where(t < cols, t, t - (rows + cols))


def _toeplitz_kernel(v_ref, o_ref):
    rows, cols = o_ref.shape[1:]
    x = jnp.broadcast_to(v_ref[0], (rows, rows + cols))
    o_ref[0] = pltpu.roll(x, 0, 1, stride=1, stride_axis=0)[:, :cols]


def _toeplitz(vecs, rows, cols):
    p = rows + cols
    lead = vecs.shape[:-1]
    n = math.prod(lead)
    out = pl.pallas_call(
        _toeplitz_kernel,
        grid=(n,),
        in_specs=[pl.BlockSpec((1, 1, p), lambda i: (i, 0, 0))],
        out_specs=pl.BlockSpec((1, rows, cols), lambda i: (i, 0, 0)),
        out_shape=jax.ShapeDtypeStruct((n, rows, cols), F32),
        compiler_params=_params("arbitrary"),
        name="toeplitz_bias",
    )(vecs.reshape(n, 1, p))
    return out.reshape(lead + (rows, cols))


def _band_bias_tiles(table_a):
    vecs = []
    for d in DILATIONS:
        delta = _toeplitz_offsets(BAND_TQ, BAND_TK) - BAND
        idx = _t5_bucket_np(delta * d)
        vecs.append(jnp.where(jnp.asarray(np.abs(delta) <= BAND)[None, :],
                              table_a.T[:, idx] * LOG2E, NEG))
    return _toeplitz(jnp.stack(vecs), BAND_TQ, BAND_TK)


def _diff_bias_tiles(table_b):
    vecs = []
    for t in range(DIFF_TILE_LO, DIFF_TILE_HI + 1):
        delta = t * DIFF_UNIT - _toeplitz_offsets(DIFF_TK, DIFF_TQ)
        vecs.append(table_b.T[:, _t5_bucket_np(delta)] * LOG2E)
    return _toeplitz(jnp.stack(vecs, axis=1), DIFF_TK, DIFF_TQ)


def _mod_kernel(c_ref, w_ref, b_ref, o_ref):
    c = c_ref[...]
    s = c * jax.nn.sigmoid(c)
    o_ref[0] = jnp.dot(s, w_ref[0], precision=lax.Precision.HIGHEST,
                       preferred_element_type=F32) + b_ref[0]


def _modulation(c_all, ada_w, ada_b):
    rows = c_all.shape[0]
    n_col = 6
    return pl.pallas_call(
        _mod_kernel,
        grid=(DEPTH, n_col),
        in_specs=[
            pl.BlockSpec((rows, D_MODEL), lambda l, j: (0, 0)),
            pl.BlockSpec((1, D_MODEL, D_MODEL), lambda l, j: (l, 0, j)),
            pl.BlockSpec((1, 1, D_MODEL), lambda l, j: (l, 0, j)),
        ],
        out_specs=pl.BlockSpec((1, rows, D_MODEL), lambda l, j: (l, 0, j)),
        out_shape=jax.ShapeDtypeStruct((DEPTH, rows, n_col * D_MODEL), F32),
        compiler_params=_params("arbitrary", "arbitrary"),
        name="adaln_modulation",
    )(c_all, ada_w, ada_b.reshape(DEPTH, 1, n_col * D_MODEL))


def _rms(x, g):
    return x * lax.rsqrt(jnp.mean(x * x, axis=-1, keepdims=True) + EPS) * g


def _inproj_kernel(x_ref, mod_ref, g_ref, wn_ref, wt_ref, *refs):
    nat, sub4, sub16 = refs[0:3], refs[3:6], refs[6:9]
    kb_ref, qbt_ref, vbt_ref, stage_ref = refs[9:13]
    tm = x_ref.shape[1]
    x = x_ref[0]
    h = _rms(x, g_ref[...]) * (1.0 + mod_ref[0, 1:2, :]) + mod_ref[0, 0:1, :]
    h = h.astype(BF16)
    for i in range(3):
        p = jnp.dot(h, wn_ref[:, i * WIDTH:(i + 1) * WIDTH], preferred_element_type=F32)
        if i == 0:
            p = p * Q_SCALE
        nat[i][0] = p.astype(BF16)
        for grp in range(WIDTH // LANES):
            lanes = slice(grp * LANES, (grp + 1) * LANES)
            stage_ref[grp] = p[:, lanes]
            for d, subs in ((DILATIONS[1], sub4), (DILATIONS[2], sub16)):
                for r in range(d):
                    subs[i][0, r, :, lanes] = stage_ref[grp, pl.ds(r, tm // d, stride=d),
                                                        :].astype(BF16)
    kb_ref[0] = jnp.dot(h, wn_ref[:, 3 * WIDTH:4 * WIDTH],
                        preferred_element_type=F32).astype(BF16)
    qt = lax.dot_general(wt_ref[0:WIDTH, :], h, NT_DIMS, preferred_element_type=F32) * Q_SCALE
    for grp in range(WIDTH // LANES):
        width = qbt_ref.shape[-1]
        for c in range(tm // width):
            qbt_ref[0, grp, c] = qt[grp * LANES:(grp + 1) * LANES,
                                    c * width:(c + 1) * width].astype(BF16)
    vt = lax.dot_general(wt_ref[WIDTH:2 * WIDTH, :], h, NT_DIMS, preferred_element_type=F32)
    for grp in range(N_HEADS_B):
        for c in range(tm // DIFF_TK):
            vbt_ref[0, grp, c, 0:LANES, :] = vt[grp * LANES:(grp + 1) * LANES,
                                                c * DIFF_TK:(c + 1) * DIFF_TK].astype(BF16)
            vbt_ref[0, grp, c, LANES:V_ROWS, :] = jnp.ones((V_ROWS - LANES, DIFF_TK), BF16)


def _inproj(x, mod, g, w_nat, w_t):
    b, l, _ = x.shape
    tm = PROJ_TILE
    groups = WIDTH // LANES
    nat = jax.ShapeDtypeStruct((b, l, WIDTH), BF16)
    nat_spec = pl.BlockSpec((1, tm, WIDTH), lambda i, j: (i, j, 0))
    sub = lambda d: jax.ShapeDtypeStruct((b, d, l // d, WIDTH), BF16)
    sub_spec = lambda d: pl.BlockSpec((1, d, tm // d, WIDTH), lambda i, j: (i, 0, j, 0))
    d4, d16 = DILATIONS[1], DILATIONS[2]
    return pl.pallas_call(
        _inproj_kernel,
        grid=(b, l // tm),
        in_specs=[
            pl.BlockSpec((1, tm, D_MODEL), lambda i, j: (i, j, 0)),
            pl.BlockSpec((1, 6, D_MODEL), lambda i, j: (i, 0, 0)),
            pl.BlockSpec((1, D_MODEL), lambda i, j: (0, 0)),
            pl.BlockSpec((D_MODEL, 4 * WIDTH), lambda i, j: (0, 0)),
            pl.BlockSpec((2 * WIDTH, D_MODEL), lambda i, j: (0, 0)),
        ],
        out_specs=[nat_spec] * 3 + [sub_spec(d4)] * 3 + [sub_spec(d16)] * 3 + [
            nat_spec,
            pl.BlockSpec((1, groups, max(tm // DIFF_TQ, 1), LANES, min(tm, DIFF_TQ)),
                         lambda i, j: (i, 0, j * tm // DIFF_TQ, 0, j % max(DIFF_TQ // tm, 1))),
            pl.BlockSpec((1, N_HEADS_B, tm // DIFF_TK, V_ROWS, DIFF_TK), lambda i, j: (i, 0, j, 0, 0)),
        ],
        out_shape=[nat] * 3 + [sub(d4)] * 3 + [sub(d16)] * 3 + [
            nat,
            jax.ShapeDtypeStruct((b, groups, l // DIFF_TQ, LANES, DIFF_TQ), BF16),
            jax.ShapeDtypeStruct((b, N_HEADS_B, l // DIFF_TK, V_ROWS, DIFF_TK), BF16),
        ],
        scratch_shapes=[pltpu.VMEM((groups, tm, LANES), F32)],
        compiler_params=_params("arbitrary", "arbitrary"),
        name="inproj",
    )(x, mod, g, w_nat, w_t)


def _lane_half_mask(half):
    lane = lax.broadcasted_iota(jnp.int32, (1, LANES), 1)
    return jnp.where(lane // HEAD_DIM == half, 1.0, 0.0).astype(BF16)


def _band_kernel(q_ref, kp_ref, kc_ref, kn_ref, vp_ref, vc_ref, vn_ref, bias_ref,
                 o_ref, lse_ref):
    j = pl.program_id(1)
    last = pl.num_programs(1) - 1
    n_sub = q_ref.shape[1] // BAND_TQ
    col = lax.broadcasted_iota(jnp.int32, (1, BAND_TK), 1)
    lo_edge = jnp.where(col < BAND, jnp.where(j == 0, NEG, 0.0), 0.0)
    hi_edge = jnp.where(col >= BAND + BAND_TQ, jnp.where(j == last, NEG, 0.0), 0.0)
    lane = lax.broadcasted_iota(jnp.int32, (1, LANES), 1)
    for g, pair in [(g, pair) for g in range(q_ref.shape[0]) for pair in range(N_HEADS_A // 2)]:
        sl = slice(pair * LANES, (pair + 1) * LANES)
        k = jnp.concatenate([kp_ref[g, :, sl], kc_ref[g, :, sl], kn_ref[g, :, sl]], axis=0)
        v = jnp.concatenate([vp_ref[g, :, sl], vc_ref[g, :, sl], vn_ref[g, :, sl]], axis=0)
        for sb in range(n_sub):
            rows = slice(sb * BAND_TQ, (sb + 1) * BAND_TQ)
            q = q_ref[g, rows, sl]
            k_sb = k[sb * BAND_TQ:sb * BAND_TQ + BAND_TK]
            v_sb = v[sb * BAND_TQ:sb * BAND_TQ + BAND_TK]
            o_pair = None
            lse_all = lse_ref[g, rows, :] if pair else jnp.zeros((BAND_TQ, LANES), F32)
            for half in range(2):
                qm = q * _lane_half_mask(half)
                s = lax.dot_general(qm, k_sb, NT_DIMS, preferred_element_type=F32)
                s = s + bias_ref[2 * pair + half]
                if sb == 0:
                    s = s + lo_edge
                if sb == n_sub - 1:
                    s = s + hi_edge
                m = jnp.max(s, axis=-1, keepdims=True)
                p = jnp.exp2(s - m)
                den = jnp.sum(p, axis=-1, keepdims=True)
                acc = jnp.dot(p.astype(BF16), v_sb, preferred_element_type=F32)
                o_h = acc / den
                lse_all = jnp.where(lane // LSE_LANES == 2 * pair + half,
                                    m + jnp.log(den) * LOG2E, lse_all)
                o_pair = o_h if half == 0 else jnp.where(lane < HEAD_DIM, o_pair, o_h)
            o_ref[g, rows, sl] = o_pair.astype(BF16)
            lse_ref[g, rows, :] = lse_all


def _band_attention(q, k, v, bias):
    s, n, _ = q.shape
    step = min(n, BAND_STEP)
    grp = BAND_STEP // step
    nq = n // step
    nk = n // BAND
    per = step // BAND
    cur = pl.BlockSpec((grp, step, WIDTH), lambda i, j: (i, j, 0))
    prev = pl.BlockSpec((grp, BAND, WIDTH), lambda i, j: (i, jnp.maximum(j * per - 1, 0), 0))
    nxt = pl.BlockSpec((grp, BAND, WIDTH), lambda i, j: (i, jnp.minimum((j + 1) * per, nk - 1), 0))
    return pl.pallas_call(
        _band_kernel,
        grid=(s // grp, nq),
        in_specs=[cur, prev, cur, nxt, prev, cur, nxt,
                  pl.BlockSpec((N_HEADS_A, BAND_TQ, BAND_TK), lambda i, j: (0, 0, 0))],
        out_specs=[cur, pl.BlockSpec((grp, step, LANES), lambda i, j: (i, j, 0))],
        out_shape=[jax.ShapeDtypeStruct((s, n, WIDTH), BF16),
                   jax.ShapeDtypeStruct((s, n, LANES), F32)],
        compiler_params=_params("arbitrary", "arbitrary"),
        name="band_attention",
    )(q, k, k, k, v, v, v, bias)


def _diff_kernel(lam_init, q0_ref, q1_ref, k0_ref, k1_ref, vt_ref, bias_ref,
                 lq1_ref, lk1_ref, lq2_ref, lk2_ref, g_ref, o_ref, a0_ref, a1_ref, s_ref):
    head = pl.program_id(0)
    qi = pl.program_id(2)
    n_kv = k0_ref.shape[1] // DIFF_TK
    row = lax.broadcasted_iota(jnp.int32, (LANES, 1), 0)
    mask = (row // HEAD_DIM == head % 2).astype(F32)
    q0t = (q0_ref[0, 0, 0].astype(F32) * mask).astype(BF16)
    q1t = (q1_ref[0, 0, 0].astype(F32) * mask).astype(BF16)
    a0_ref[...] = jnp.zeros(a0_ref.shape, F32)
    a1_ref[...] = jnp.zeros(a1_ref.shape, F32)
    maps = ((q0t, k0_ref, a0_ref), (q1t, k1_ref, a1_ref))

    def scores(kk, slot):
        start = pl.multiple_of(kk * DIFF_TK, DIFF_TK)
        off = kk * (DIFF_TK // DIFF_UNIT) - qi * (DIFF_TQ // DIFF_UNIT)
        bias = bias_ref[0, jnp.clip(off, DIFF_TILE_LO, DIFF_TILE_HI) - DIFF_TILE_LO]
        cmax = []
        for m, (qt, k_ref, _) in enumerate(maps):
            s = jnp.dot(k_ref[0, pl.ds(start, DIFF_TK), :], qt, preferred_element_type=F32) + bias
            s_ref[slot, m] = s
            cmax.append(jnp.max(s, axis=0, keepdims=True))
        return tuple(cmax)

    def consume(kk, slot, cmax, stats):
        vt = vt_ref[0, 0, kk]
        out = []
        for m, (_, _, a_ref) in enumerate(maps):
            m_new = jnp.maximum(stats[m], cmax[m])
            alpha = jnp.exp2(stats[m] - m_new)
            p = jnp.exp2(s_ref[slot, m] - m_new)
            a_ref[...] = alpha * a_ref[...] + jnp.dot(vt, p.astype(BF16),
                                                      preferred_element_type=F32)
            out.append(m_new)
        return tuple(out)

    def pair(j, carry):
        cmax, stats = carry
        nxt = scores(2 * j + 1, 1)
        stats = consume(2 * j, 0, cmax, stats)
        cmax = scores(2 * j + 2, 0)
        stats = consume(2 * j + 1, 1, nxt, stats)
        return cmax, stats

    stat = jnp.full((1, DIFF_TQ), NEG, F32)

    cmax, stats = lax.fori_loop(0, n_kv // 2 - 1, pair, (scores(0, 0), (stat, stat)))
    nxt = scores(n_kv - 1, 1)
    stats = consume(n_kv - 2, 0, cmax, stats)
    consume(n_kv - 1, 1, nxt, stats)

    lam = (jnp.exp(jnp.sum(lq1_ref[...] * lk1_ref[...], axis=-1, keepdims=True))
           - jnp.exp(jnp.sum(lq2_ref[...] * lk2_ref[...], axis=-1, keepdims=True))
           + lam_init)
    l0 = a0_ref[LANES:LANES + 1, :]
    l1 = a1_ref[LANES:LANES + 1, :]
    o = (a0_ref[0:LANES, :] / l0 - lam * (a1_ref[0:LANES, :] / l1)).T
    o_ref[0] = (_rms(o, g_ref[...]) * (1.0 - lam_init)).astype(BF16)


def _diff_attention(qbt, kb, vbt, bias, lq1, lk1, lq2, lk2, g, lam_init):
    b, l, _ = kb.shape
    nq = l // DIFF_TQ
    nk = l // DIFF_TK
    n_tiles = bias.shape[1]
    pairs = N_HEADS_B // 2
    qspec = lambda m: pl.BlockSpec((1, 1, 1, LANES, DIFF_TQ),
                                   lambda h, i, j: (i, m * pairs + h // 2, j, 0, 0))
    kspec = lambda m: pl.BlockSpec((1, l, LANES), lambda h, i, j: (i, 0, m * pairs + h // 2))
    small = pl.BlockSpec((1, HEAD_DIM), lambda h, i, j: (0, 0))
    acc = pltpu.VMEM((V_ROWS, DIFF_TQ), F32)
    return pl.pallas_call(
        functools.partial(_diff_kernel, lam_init),
        grid=(N_HEADS_B, b, nq),
        in_specs=[
            qspec(0), qspec(1), kspec(0), kspec(1),
            pl.BlockSpec((1, 1, nk, V_ROWS, DIFF_TK), lambda h, i, j: (i, h, 0, 0, 0)),
            pl.BlockSpec((1, n_tiles, DIFF_TK, DIFF_TQ), lambda h, i, j: (h, 0, 0, 0),
                         pipeline_mode=pl.Buffered(1)),
            small, small, small, small,
            pl.BlockSpec((1, 2 * HEAD_DIM), lambda h, i, j: (0, 0)),
        ],
        out_specs=pl.BlockSpec((1, DIFF_TQ, LANES), lambda h, i, j: (i, j, h)),
        out_shape=jax.ShapeDtypeStruct((b, l, WIDTH), BF16),
        scratch_shapes=[acc, acc, pltpu.VMEM((2, 2, DIFF_TK, DIFF_TQ), F32)],
        compiler_params=_params("arbitrary", "arbitrary", "arbitrary"),
        name="diff_attention",
    )(qbt, qbt, kb, kb, vbt, bias, lq1, lk1, lq2, lk2, g)


def _mix_residual(x_ref, o1_ref, l1_ref, o4_ref, l4_ref, o16_ref, l16_ref, ob_ref,
                  mod_ref, w_ref, so4_ref, sl4_ref, so16_ref, sl16_ref, oa_ref):
    tm = x_ref.shape[1]
    for d, src, dst in ((DILATIONS[1], l4_ref, sl4_ref), (DILATIONS[2], l16_ref, sl16_ref)):
        for r in range(d):
            dst[pl.ds(r, tm // d, stride=d), :] = src[0, r]
    l1, l2, l3 = l1_ref[0], sl4_ref[...], sl16_ref[...]
    mx = jnp.maximum(jnp.maximum(l1, l2), l3)
    e1, e2, e3 = jnp.exp2(l1 - mx), jnp.exp2(l2 - mx), jnp.exp2(l3 - mx)
    inv = 1.0 / (e1 + e2 + e3)
    w1, w2, w3 = e1 * inv, e2 * inv, e3 * inv
    lane = lax.broadcasted_iota(jnp.int32, (1, LANES), 1)
    for grp in range(WIDTH // LANES):
        lanes = slice(grp * LANES, (grp + 1) * LANES)
        for d, src, dst in ((DILATIONS[1], o4_ref, so4_ref), (DILATIONS[2], o16_ref, so16_ref)):
            for r in range(d):
                dst[grp, pl.ds(r, tm // d, stride=d), :] = src[0, r, :, lanes].astype(F32)
        lo, hi = 2 * grp * LSE_LANES, (2 * grp + 1) * LSE_LANES
        wide = lambda w: jnp.where(lane < HEAD_DIM, w[:, lo:lo + 1], w[:, hi:hi + 1])
        oa = (wide(w1) * o1_ref[0, :, lanes].astype(F32) + wide(w2) * so4_ref[grp]
              + wide(w3) * so16_ref[grp])
        oa_ref[:, lanes] = oa.astype(BF16)
    mix = (jnp.dot(oa_ref[...], w_ref[0:WIDTH, :], preferred_element_type=F32)
           + jnp.dot(ob_ref[0], w_ref[WIDTH:2 * WIDTH, :], preferred_element_type=F32))
    return x_ref[0] + mod_ref[0, 2:3, :] * mix


def _ffn_residual(final, x, mod_ref, g_ref, wg_ref, wu_ref, wd_ref, fg_ref, act_ref):
    h = _rms(x, g_ref[...]) * (1.0 + mod_ref[0, 4:5, :]) + mod_ref[0, 3:4, :]
    h = h.astype(BF16)
    for c in range(D_FF // FF_CHUNK):
        sl = slice(c * FF_CHUNK, (c + 1) * FF_CHUNK)
        gate = jnp.dot(h, wg_ref[:, sl], preferred_element_type=F32)
        up = jnp.dot(h, wu_ref[:, sl], preferred_element_type=F32)
        act_ref[:, sl] = (gate * jax.nn.sigmoid(gate) * up).astype(BF16)
    ff = jnp.dot(act_ref[...], wd_ref[...], preferred_element_type=F32)
    y = x + mod_ref[0, 5:6, :] * ff
    return _rms(y, fg_ref[...]) if final else y


def _mix_ffn_kernel(final, x_ref, o1_ref, l1_ref, o4_ref, l4_ref, o16_ref, l16_ref, ob_ref,
                    mod_ref, w_ref, g_ref, wg_ref, wu_ref, wd_ref, fg_ref, y_ref,
                    so4_ref, sl4_ref, so16_ref, sl16_ref, oa_ref, act_ref):
    x1 = _mix_residual(x_ref, o1_ref, l1_ref, o4_ref, l4_ref, o16_ref, l16_ref, ob_ref,
                       mod_ref, w_ref, so4_ref, sl4_ref, so16_ref, sl16_ref, oa_ref)
    y_ref[0] = _ffn_residual(final, x1, mod_ref, g_ref, wg_ref, wu_ref, wd_ref, fg_ref, act_ref)


def _mix_ffn(x, band_out, ob, mod, w_out, g, w_gate, w_up, w_down, final_g, final):
    b, l, _ = x.shape
    tm = PROJ_TILE
    xs = pl.BlockSpec((1, tm, D_MODEL), lambda i, j: (i, j, 0))
    hs = pl.BlockSpec((1, tm, WIDTH), lambda i, j: (i, j, 0))
    ls = pl.BlockSpec((1, tm, LANES), lambda i, j: (i, j, 0))
    sub = lambda d, w: pl.BlockSpec((1, d, tm // d, w), lambda i, j: (i, 0, j, 0))
    vec = pl.BlockSpec((1, D_MODEL), lambda i, j: (0, 0))
    once = lambda shape: pl.BlockSpec(shape, lambda i, j: (0, 0), pipeline_mode=pl.Buffered(1))
    d4, d16 = DILATIONS[1], DILATIONS[2]
    stage_o = pltpu.VMEM((WIDTH // LANES, tm, LANES), F32)
    stage_l = pltpu.VMEM((tm, LANES), F32)
    (o1, l1), (o4, l4), (o16, l16) = band_out
    return pl.pallas_call(
        functools.partial(_mix_ffn_kernel, final),
        grid=(b, l // tm),
        in_specs=[xs, hs, ls, sub(d4, WIDTH), sub(d4, LANES), sub(d16, WIDTH), sub(d16, LANES), hs,
                  pl.BlockSpec((1, 6, D_MODEL), lambda i, j: (i, 0, 0)),
                  once((2 * WIDTH, D_MODEL)), vec,
                  once((D_MODEL, D_FF)), once((D_MODEL, D_FF)), once((D_FF, D_MODEL)), vec],
        out_specs=xs,
        out_shape=jax.ShapeDtypeStruct(x.shape, F32),
        scratch_shapes=[stage_o, stage_l, stage_o, stage_l, pltpu.VMEM((tm, WIDTH), BF16),
                        pltpu.VMEM((tm, D_FF), BF16)],
        compiler_params=_params("arbitrary", "arbitrary"),
        name="mix_ffn",
    )(x, o1, l1, o4, l4, o16, l16, ob, mod, w_out, g, w_gate, w_up, w_down, final_g)


def _trunk(x, mods, band_bias, diff_bias, p):
    b, l, _ = x.shape
    for layer in range(DEPTH):
        mod = mods[layer]
        outs = _inproj(x, mod, p["norm_mix_g"][layer:layer + 1], p["w_nat"][layer], p["w_t"][layer])
        kb, qbt, vbt = outs[9:12]
        band_out = []
        for i, d in enumerate(DILATIONS):
            q, k, v = (t.reshape(b * d, l // d, WIDTH) for t in outs[3 * i:3 * i + 3])
            o, lse = _band_attention(q, k, v, band_bias[i])
            lead = (b, l) if d == 1 else (b, d, l // d)
            band_out.append((o.reshape(lead + (WIDTH,)), lse.reshape(lead + (LANES,))))
        lam_init = 0.8 - 0.6 * math.exp(-0.3 * layer)
        ob = _diff_attention(qbt, kb, vbt, diff_bias,
                             p["lambda_q1"][layer:layer + 1], p["lambda_k1"][layer:layer + 1],
                             p["lambda_q2"][layer:layer + 1], p["lambda_k2"][layer:layer + 1],
                             p["subln_g"][layer:layer + 1], lam_init)
        x = _mix_ffn(x, band_out, ob, mod, p["w_out"][layer], p["norm_ffn_g"][layer:layer + 1],
                     p["w_gate"][layer], p["w_up"][layer], p["w_down"][layer], p["final_g"],
                     final=(layer == DEPTH - 1))
    return x


def kernel(x_prompt, x_sample, c_prompt, c_sample, rel_bias, ada_w, ada_b, norm_mix_g,
           norm_ffn_g, w_in, w_out, lambda_q1, lambda_k1, lambda_q2, lambda_k2, subln_g,
           w_gate, w_up, w_down, final_g):
    nb_p, nb_s = c_prompt.shape[0], c_sample.shape[0]
    rows = -(-(nb_p + nb_s) // 8) * 8
    c_all = jnp.zeros((rows, D_MODEL), F32).at[:nb_p].set(c_prompt).at[nb_p:nb_p + nb_s].set(c_sample)
    mods = _modulation(c_all, ada_w, ada_b).reshape(DEPTH, rows, 6, D_MODEL)
    band_bias = _band_bias_tiles(rel_bias[:, :N_HEADS_A])
    diff_bias = _diff_bias_tiles(rel_bias[:, N_HEADS_A:])
    w_bf = w_in.astype(BF16)
    w_nat = jnp.concatenate([w_bf[:, :, :3 * WIDTH], w_bf[:, :, 4 * WIDTH:5 * WIDTH]], axis=2)
    w_t = jnp.concatenate([w_bf[:, :, 3 * WIDTH:4 * WIDTH], w_bf[:, :, 5 * WIDTH:]],
                          axis=2).transpose(0, 2, 1)
    p = dict(norm_mix_g=norm_mix_g, norm_ffn_g=norm_ffn_g, w_nat=w_nat, w_t=w_t,
             w_out=w_out.astype(BF16), lambda_q1=lambda_q1, lambda_k1=lambda_k1,
             lambda_q2=lambda_q2, lambda_k2=lambda_k2, subln_g=subln_g,
             w_gate=w_gate.astype(BF16), w_up=w_up.astype(BF16), w_down=w_down.astype(BF16),
             final_g=final_g.reshape(1, D_MODEL))
    y_prompt = _trunk(x_prompt, mods[:, :nb_p], band_bias, diff_bias, p)
    y_sample = _trunk(x_sample, mods[:, nb_p:nb_p + nb_s], band_bias, diff_bias, p)
    return (y_prompt, y_sample)
```
